```python
import functools
import jax
import jax.numpy as jnp
from jax import lax
import numpy as np

D_MODEL = 1024
BATCH = 8
SEQ = 4096
DEPTH = 2
DEC_BATCH = 32
DEC_SEQ = 1
PAST_LEN = 16384
PAGE_SIZE = 128

HEAD_DIM = 64
N_HEADS = 8
MIX_W = N_HEADS * HEAD_DIM
MEM_HEADS = 4
MEM_W = MEM_HEADS * HEAD_DIM
N_MEM = 256
DECAY_RANK = 64
ICLR_RANK = 64
N_BRANCH = 4
Q_BLOCK = 128
NORM_EPS = 1e-6
GN_EPS = 64e-5
ATTN_SCALE = HEAD_DIM ** -0.5
RWKV_SHIFT_W = 3 * MIX_W + DECAY_RANK + ICLR_RANK
IN_SPLITS = (RWKV_SHIFT_W, MIX_W,
             MIX_W, MIX_W, MIX_W, MIX_W,
             MIX_W, MIX_W, MIX_W, MIX_W, N_HEADS,
             MEM_W, MEM_W,
             N_BRANCH * D_MODEL)
D_IN = sum(IN_SPLITS)

kernel_name = 'hybrid_rwkv7_stickbreak_fox_memory_step'


def _rmsnorm(x, g):
    xf = x.astype(jnp.float32)
    y = xf * lax.rsqrt(jnp.mean(xf * xf, axis=-1, keepdims=True) + NORM_EPS)
    return (y * g.astype(jnp.float32)).astype(x.dtype)


def _split_in(xn, w_in_l):
    z = jnp.einsum('btd,de->bte', xn, w_in_l)
    cuts = [int(c) for c in np.cumsum(IN_SPLITS)[:-1]]
    return jnp.split(z, cuts, axis=-1)


def _heads(t):
    return t.reshape(t.shape[:2] + (-1, HEAD_DIM))


def _gate_out(o, gate):
    return o.reshape(gate.shape).astype(gate.dtype) * jax.nn.silu(gate)


def _wkv_scan(s0, r, k, v, w, kk, a):
    def step(s, inp):
        r_t, k_t, v_t, w_t, kk_t, a_t = inp
        sa = jnp.einsum('bhij,bhj->bhi', s, -kk_t)
        s = (s * w_t[:, :, None, :] + sa[..., None] * (kk_t * a_t)[:, :, None, :]
             + v_t[..., None] * k_t[:, :, None, :])
        return s, jnp.einsum('bhij,bhj->bhi', s, r_t)
    xs = tuple(jnp.swapaxes(t, 0, 1) for t in (r, k, v, w, kk, a))
    s_last, ys = lax.scan(step, s0, xs)
    return s_last, jnp.swapaxes(ys, 0, 1)


def _rwkv7_branch(z_sh, gate, s0, shift0, mu, w0, w_up, a0, a_up, k_k, k_a, r_k, ln_w, ln_b):
    f32 = jnp.float32
    prev = jnp.concatenate([shift0[:, None, :].astype(z_sh.dtype), z_sh[:, :-1]], axis=1)
    zm = z_sh + (prev - z_sh) * mu
    r, k, v, wd, ad = jnp.split(zm, [MIX_W, 2 * MIX_W, 3 * MIX_W, 3 * MIX_W + DECAY_RANK], axis=-1)
    w_log = -jax.nn.softplus(-(w0 + jnp.tanh(wd) @ w_up).astype(f32)) - 0.5
    decay = _heads(jnp.exp(-jnp.exp(w_log)))
    a = _heads(jax.nn.sigmoid((a0 + ad @ a_up).astype(f32)))
    r, k, v = _heads(r).astype(f32), _heads(k).astype(f32), _heads(v).astype(f32)
    kk = k * k_k.reshape(N_HEADS, HEAD_DIM).astype(f32)
    kk = kk / jnp.maximum(jnp.sqrt(jnp.sum(kk * kk, axis=-1, keepdims=True)), 1e-12)
    k = k * (1.0 + (a - 1.0) * k_a.reshape(N_HEADS, HEAD_DIM).astype(f32))
    s_last, y = _wkv_scan(s0.astype(f32), r, k, v, decay, kk, a)
    mean = jnp.mean(y, axis=-1, keepdims=True)
    var = jnp.mean(jnp.square(y - mean), axis=-1, keepdims=True)
    yn = ((y - mean) * lax.rsqrt(var + GN_EPS) * ln_w.reshape(N_HEADS, HEAD_DIM).astype(f32)
          + ln_b.reshape(N_HEADS, HEAD_DIM).astype(f32))
    bonus = jnp.sum(r * k * r_k.astype(f32), axis=-1, keepdims=True) * v
    return _gate_out(yn + bonus, gate), s_last, z_sh[:, -1]


def _stick_breaking_weights(z, mask):
    sp = jnp.where(mask, jax.nn.softplus(z), 0.0)
    suffix = lax.cumsum(sp, axis=z.ndim - 1, reverse=True) - sp
    return jnp.where(mask, jnp.exp(jax.nn.log_sigmoid(z) - suffix), 0.0)


def _sb_prompt(q, k, v):
    b, s = q.shape[:2]
    nb = s // Q_BLOCK
    kpos = jnp.arange(s)
    qb = jnp.moveaxis(q.reshape(b, nb, Q_BLOCK, N_HEADS, HEAD_DIM), 1, 0)

    def one_block(args):
        i, q_i = args
        qpos = i * Q_BLOCK + jnp.arange(Q_BLOCK)
        z = jnp.einsum('bqhd,bshd->bhqs', q_i, k).astype(jnp.float32) * ATTN_SCALE
        a = _stick_breaking_weights(z, kpos[None, :] < qpos[:, None])
        return jnp.einsum('bhqs,bshd->bqhd', a.astype(v.dtype), v)

    o = lax.map(one_block, (jnp.arange(nb), qb))
    return jnp.moveaxis(o, 0, 1).reshape(b, s, N_HEADS, HEAD_DIM)


def _sb_sample(k_past, v_past, q, k_new, v_new):
    p_len, t = k_past.shape[1], q.shape[1]
    z = jnp.concatenate([jnp.einsum('bqhd,bshd->bhqs', q, k_past),
                         jnp.einsum('bqhd,bshd->bhqs', q, k_new)], axis=-1).astype(jnp.float32) * ATTN_SCALE
    mask = jnp.arange(p_len + t)[None, :] < (p_len + jnp.arange(t))[:, None]
    a = _stick_breaking_weights(z, mask).astype(v_new.dtype)
    return (jnp.einsum('bhqs,bshd->bqhd', a[..., :p_len], v_past)
            + jnp.einsum('bhqs,bshd->bqhd', a[..., p_len:], v_new))


def _fox_prompt(q, k, v, logf):
    b, s = q.shape[:2]
    nb = s // Q_BLOCK
    fk = jnp.swapaxes(jnp.cumsum(logf, axis=1), 1, 2)
    kpos = jnp.arange(s)
    qb = jnp.moveaxis(q.reshape(b, nb, Q_BLOCK, N_HEADS, HEAD_DIM), 1, 0)
    fqb = jnp.moveaxis(fk.reshape(b, N_HEADS, nb, Q_BLOCK), 2, 0)

    def one_block(args):
        i, q_i, fq_i = args
        qpos = i * Q_BLOCK + jnp.arange(Q_BLOCK)
        z = (jnp.einsum('bqhd,bshd->bhqs', q_i, k).astype(jnp.float32) * ATTN_SCALE
             + fq_i[..., None] - fk[:, :, None, :])
        z = jnp.where(kpos[None, :] <= qpos[:, None], z, -jnp.inf)
        p = jax.nn.softmax(z, axis=-1)
        return jnp.einsum('bhqs,bshd->bqhd', p.astype(v.dtype), v)

    o = lax.map(one_block, (jnp.arange(nb), qb, fqb))
    return jnp.moveaxis(o, 0, 1).reshape(b, s, N_HEADS, HEAD_DIM)


def _fox_sample(k_past, v_past, logf_past, q, k_new, v_new, logf_new):
    p_len, t = k_past.shape[1], q.shape[1]
    fcum = jnp.cumsum(jnp.concatenate([logf_past.astype(jnp.float32), logf_new], axis=1), axis=1)
    fk = jnp.swapaxes(fcum, 1, 2)
    z = jnp.concatenate([jnp.einsum('bqhd,bshd->bhqs', q, k_past),
                         jnp.einsum('bqhd,bshd->bhqs', q, k_new)], axis=-1).astype(jnp.float32) * ATTN_SCALE
    z = z + fk[:, :, p_len:, None] - fk[:, :, None, :]
    mask = jnp.arange(p_len + t)[None, :] <= (p_len + jnp.arange(t))[:, None]
    p = jax.nn.softmax(jnp.where(mask, z, -jnp.inf), axis=-1).astype(v_new.dtype)
    return (jnp.einsum('bhqs,bshd->bqhd', p[..., :p_len], v_past)
            + jnp.einsum('bhqs,bshd->bqhd', p[..., p_len:], v_new))


def _memory_kv(mem, g, w_kv):
    kv = jnp.einsum('bmd,de->bme', _rmsnorm(mem, g), w_kv)
    mk, mv = jnp.split(kv, 2, axis=-1)
    return _heads(mk), _heads(mv)


def _memory_attend(q, mk, mv):
    z = jnp.einsum('bqhd,bmhd->bhqm', q, mk).astype(jnp.float32) * ATTN_SCALE
    p = jax.nn.softmax(z, axis=-1).astype(mv.dtype)
    return jnp.einsum('bhqm,bmhd->bqhd', p, mv)


def _layer(x, norm_g_l, w_in_l, rwkv_p, fbias_l, w_br, w_out_l, mem_k, mem_v, s0, shift0, sb_fn, fox_fn):
    xn = _rmsnorm(x, norm_g_l)
    (z_rw, g_rw, sb_q, sb_k, sb_v, g_sb, fx_q, fx_k, fx_v, g_fx, fx_f,
     m_q, g_m, z_gate) = _split_in(xn, w_in_l)
    y_rw, s_last, shift_last = _rwkv7_branch(z_rw, g_rw, s0, shift0, *rwkv_p)
    sb_k, sb_v = _heads(sb_k), _heads(sb_v)
    y_sb = _gate_out(sb_fn(_heads(sb_q), sb_k, sb_v), g_sb)
    fx_k, fx_v = _heads(fx_k), _heads(fx_v)
    logf = jax.nn.log_sigmoid((fx_f + fbias_l).astype(jnp.float32))
    y_fx = _gate_out(fox_fn(_heads(fx_q), fx_k, fx_v, logf), g_fx)
    y_m = _gate_out(_memory_attend(_heads(m_q), mem_k, mem_v), g_m)
    gates = jax.nn.sigmoid(z_gate.astype(jnp.float32)).astype(x.dtype)
    gates = gates.reshape(x.shape[:2] + (N_BRANCH, D_MODEL))
    merged = (gates[:, :, 0] * (y_rw @ w_br[0]) + gates[:, :, 1] * (y_sb @ w_br[1])
              + gates[:, :, 2] * (y_fx @ w_br[2]) + gates[:, :, 3] * (y_m @ w_br[3]))
    x = x + merged @ w_out_l
    return x, (sb_k, sb_v, fx_k, fx_v, logf, s_last, shift_last)


def setup_inputs(seed: int = 0) -> dict:
    key = jax.random.key(seed)
    ks = iter(jax.random.split(key, 48))
    f32 = jnp.float32

    def nrm(shape, scale=1.0):
        return jax.random.normal(next(ks), shape, f32) * scale

    n_pages = PAST_LEN // PAGE_SIZE
    n_used = DEC_BATCH * n_pages
    n_pool = n_used + max(1, n_used // 4)
    pool_kv = (DEPTH, n_pool, PAGE_SIZE, N_HEADS, HEAD_DIM)
    x_prompt = nrm((BATCH, SEQ, D_MODEL))
    x_sample = nrm((DEC_BATCH, DEC_SEQ, D_MODEL))
    cache_sb_k = nrm(pool_kv)
    cache_sb_v = nrm(pool_kv)
    cache_fox_k = nrm(pool_kv)
    cache_fox_v = nrm(pool_kv)
    cache_fox_logf = jax.nn.log_sigmoid(4.0 + nrm((DEPTH, n_pool, PAGE_SIZE, N_HEADS), 0.5))
    cache_mem_k = nrm((DEPTH, DEC_BATCH, N_MEM, MEM_HEADS, HEAD_DIM))
    cache_mem_v = nrm((DEPTH, DEC_BATCH, N_MEM, MEM_HEADS, HEAD_DIM))
    state_rwkv_wkv = nrm((DEPTH, DEC_BATCH, N_HEADS, HEAD_DIM, HEAD_DIM), 0.3)
    state_rwkv_shift = nrm((DEPTH, DEC_BATCH, RWKV_SHIFT_W))
    page_table = jax.random.permutation(next(ks), n_pool)[:n_used].reshape(DEC_BATCH, n_pages).astype(jnp.int32)
    mem_prompt = nrm((BATCH, N_MEM, D_MODEL))
    return {
        'x_prompt': x_prompt, 'x_sample': x_sample,
        'cache_sb_k': cache_sb_k, 'cache_sb_v': cache_sb_v,
        'cache_fox_k': cache_fox_k, 'cache_fox_v': cache_fox_v, 'cache_fox_logf': cache_fox_logf,
        'cache_mem_k': cache_mem_k, 'cache_mem_v': cache_mem_v,
        'state_rwkv_wkv': state_rwkv_wkv, 'state_rwkv_shift': state_rwkv_shift,
        'page_table': page_table, 'mem_prompt': mem_prompt,
        'norm_g': 1.0 + nrm((DEPTH, D_MODEL), 0.05),
        'w_in': nrm((DEPTH, D_MODEL, D_IN), D_MODEL ** -0.5),
        'rwkv_mu': jax.random.uniform(next(ks), (DEPTH, RWKV_SHIFT_W), f32),
        'rwkv_w0': -2.0 + nrm((DEPTH, MIX_W), 1.0),
        'rwkv_w_up': nrm((DEPTH, DECAY_RANK, MIX_W), 0.5 * DECAY_RANK ** -0.5),
        'rwkv_a0': nrm((DEPTH, MIX_W), 0.5),
        'rwkv_a_up': nrm((DEPTH, ICLR_RANK, MIX_W), 0.5 * ICLR_RANK ** -0.5),
        'rwkv_k_k': 0.85 + nrm((DEPTH, MIX_W), 0.05),
        'rwkv_k_a': 1.0 + nrm((DEPTH, MIX_W), 0.05),
        'rwkv_r_k': nrm((DEPTH, N_HEADS, HEAD_DIM), 0.1),
        'rwkv_ln_w': 1.0 + nrm((DEPTH, MIX_W), 0.05),
        'rwkv_ln_b': nrm((DEPTH, MIX_W), 0.02),
        'fox_fbias': 4.0 + nrm((DEPTH, N_HEADS), 0.5),
        'mem_norm_g': 1.0 + nrm((DEPTH, D_MODEL), 0.05),
        'w_mem_kv': nrm((DEPTH, D_MODEL, 2 * MEM_W), D_MODEL ** -0.5),
        'w_rwkv_o': nrm((DEPTH, MIX_W, D_MODEL), MIX_W ** -0.5),
        'w_sb_o': nrm((DEPTH, MIX_W, D_MODEL), MIX_W ** -0.5),
        'w_fox_o': nrm((DEPTH, MIX_W, D_MODEL), MIX_W ** -0.5),
        'w_mem_o': nrm((DEPTH, MEM_W, D_MODEL), MEM_W ** -0.5),
        'w_out': nrm((DEPTH, D_MODEL, D_MODEL), D_MODEL ** -0.5),
        'final_norm_g': 1.0 + nrm((D_MODEL,), 0.05),
    }


def reference(x_prompt, x_sample, cache_sb_k, cache_sb_v, cache_fox_k, cache_fox_v,
              cache_fox_logf, cache_mem_k, cache_mem_v, state_rwkv_wkv, state_rwkv_shift,
              page_table, mem_prompt, norm_g, w_in, rwkv_mu, rwkv_w0, rwkv_w_up, rwkv_a0,
              rwkv_a_up, rwkv_k_k, rwkv_k_a, rwkv_r_k, rwkv_ln_w, rwkv_ln_b, fox_fbias,
              mem_norm_g, w_mem_kv, w_rwkv_o, w_sb_o, w_fox_o, w_mem_o, w_out, final_norm_g):
    bp = x_prompt.shape[0]
    bs = x_sample.shape[0]
    p_len = page_table.shape[1] * cache_sb_k.shape[2]
    flat_pages = page_table.reshape(-1)

    def gather(pool):
        return pool[flat_pages].reshape((bs, p_len) + pool.shape[2:])

    hp, hs = x_prompt, x_sample
    p_new = [[] for _ in range(9)]
    s_new = [[] for _ in range(7)]
    for l in range(DEPTH):
        rwkv_p = (rwkv_mu[l], rwkv_w0[l], rwkv_w_up[l], rwkv_a0[l], rwkv_a_up[l],
                  rwkv_k_k[l], rwkv_k_a[l], rwkv_r_k[l], rwkv_ln_w[l], rwkv_ln_b[l])
        w_br = (w_rwkv_o[l], w_sb_o[l], w_fox_o[l], w_mem_o[l])
        mk, mv = _memory_kv(mem_prompt, mem_norm_g[l], w_mem_kv[l])
        s0 = jnp.zeros((bp, N_HEADS, HEAD_DIM, HEAD_DIM), jnp.float32)
        sh0 = jnp.zeros((bp, RWKV_SHIFT_W), hp.dtype)
        hp, st = _layer(hp, norm_g[l], w_in[l], rwkv_p, fox_fbias[l], w_br, w_out[l],
                        mk, mv, s0, sh0, _sb_prompt, _fox_prompt)
        for lst, arr in zip(p_new, st + (mk, mv)):
            lst.append(arr)
        sb_fn = functools.partial(_sb_sample, gather(cache_sb_k[l]), gather(cache_sb_v[l]))
        fox_fn = functools.partial(_fox_sample, gather(cache_fox_k[l]), gather(cache_fox_v[l]),
                                   gather(cache_fox_logf[l]))
        hs, st = _layer(hs, norm_g[l], w_in[l], rwkv_p, fox_fbias[l], w_br, w_out[l],
                        cache_mem_k[l], cache_mem_v[l], state_rwkv_wkv[l], state_rwkv_shift[l],
                        sb_fn, fox_fn)
        for lst, arr in zip(s_new, st):
            lst.append(arr)
    y_prompt = _rmsnorm(hp, final_norm_g)
    y_sample = _rmsnorm(hs, final_norm_g)
    (p_sb_k, p_sb_v, p_fox_k, p_fox_v, p_fox_logf, p_rwkv_wkv, p_rwkv_shift,
     p_mem_k, p_mem_v) = (jnp.stack(t) for t in p_new)
    (s_sb_k, s_sb_v, s_fox_k, s_fox_v, s_fox_logf, s_rwkv_wkv,
     s_rwkv_shift) = (jnp.stack(t) for t in s_new)
    return (y_prompt, y_sample, p_sb_k, p_sb_v, p_fox_k, p_fox_v, p_fox_logf, p_rwkv_wkv,
            p_rwkv_shift, p_mem_k, p_mem_v, s_sb_k, s_sb_v, s_fox_k, s_fox_v, s_fox_logf,
            s_rwkv_wkv, s_rwkv_shift)
```

```python
import functools

import jax
import jax.numpy as jnp
from jax import lax
from jax.experimental import pallas as pl
from jax.experimental.pallas import tpu as pltpu

F32 = jnp.float32
BF16 = jnp.bfloat16

D_MODEL = 1024
HEAD_DIM = 64
N_HEADS = 8
MIX_W = N_HEADS * HEAD_DIM
MEM_HEADS = 4
MEM_W = MEM_HEADS * HEAD_DIM
DECAY_RANK = 64
ICLR_RANK = 64
N_BRANCH = 4
NORM_EPS = 1e-6
GN_EPS = 64e-5
ATTN_SCALE = HEAD_DIM ** -0.5
RWKV_SHIFT_W = 3 * MIX_W + DECAY_RANK + ICLR_RANK

LANES = 128
SUBLANES = 8
VMEM_LIMIT = 48 * 1024 * 1024

RW_BLOCK_W = 2048
C_RW = 0
C_FXF = RWKV_SHIFT_W
C_GRW = 2048
C_SBQ, C_SBK, C_SBV, C_GSB = 2560, 3072, 3584, 4096
C_FXQ, C_FXK, C_FXV, C_GFX = 4608, 5120, 5632, 6144
C_MQ, C_GM = 6656, 6912
C_GATE = 7168
N_PROJ = C_GATE + N_BRANCH * D_MODEL


def _cparams(sem):
    return pltpu.CompilerParams(dimension_semantics=sem, vmem_limit_bytes=VMEM_LIMIT)


def _split3(x):
    hi = x.astype(BF16)
    r1 = x - hi.astype(F32)
    mid = r1.astype(BF16)
    lo = (r1 - mid.astype(F32)).astype(BF16)
    return hi, mid, lo


def _dot(a, b):
    return jnp.dot(a, b, preferred_element_type=F32)


def _dot_nt(a, b):
    return lax.dot_general(a, b, (((1,), (1,)), ((), ())), preferred_element_type=F32)


def _dot3(x, ones):
    hi, mid, lo = _split3(x)
    return _dot(hi, ones) + _dot(mid, ones) + _dot(lo, ones)


def _softplus(x):
    return jnp.maximum(x, 0.0) + jnp.log(1.0 + jnp.exp(-jnp.abs(x)))


def _sigmoid(x):
    return 1.0 / (1.0 + jnp.exp(-x))


def _silu(x):
    return x * _sigmoid(x)


def _head_ones(width):
    r = lax.broadcasted_iota(jnp.int32, (width, width), 0) // HEAD_DIM
    c = lax.broadcasted_iota(jnp.int32, (width, width), 1) // HEAD_DIM
    return (r == c).astype(BF16)


def _proj_kernel(x_ref, g_ref, w_ref, o_ref, xn_ref):
    @pl.when(pl.program_id(1) == 0)
    def _():
        x = x_ref[...]
        ms = jnp.mean(x * x, axis=-1, keepdims=True)
        xn_ref[...] = (x * lax.rsqrt(ms + NORM_EPS) * g_ref[...]).astype(BF16)

    o_ref[...] = _dot(xn_ref[...], w_ref[...])


def _norm_proj(x2d, g, w_bf16, tn):
    m, d = x2d.shape
    n = w_bf16.shape[1]
    tm = min(m, 1024)
    return pl.pallas_call(
        _proj_kernel,
        out_shape=jax.ShapeDtypeStruct((m, n), F32),
        grid=(m // tm, n // tn),
        in_specs=[
            pl.BlockSpec((tm, d), lambda i, j: (i, 0)),
            pl.BlockSpec((1, d), lambda i, j: (0, 0)),
            pl.BlockSpec((d, tn), lambda i, j: (0, j)),
        ],
        out_specs=pl.BlockSpec((tm, tn), lambda i, j: (i, j)),
        scratch_shapes=[pltpu.VMEM((tm, d), BF16)],
        compiler_params=_cparams(("parallel", "arbitrary")),
        name="norm_proj",
    )(x2d, g.reshape(1, d), w_bf16)


def _rwkv_prep_kernel(z_ref, zprev_ref, sh0_ref, mu_ref, w0_ref, a0_ref, kk_ref, ka_ref, rk_ref,
                      wup_ref, aup_ref, ones_ref,
                      r_out, k_out, v_out, w_out, kn_out, b_out, bonus_out, *, ts, seq):
    z = z_ref[0, :, 0:RWKV_SHIFT_W]
    if seq:
        first = jnp.where(pl.program_id(1) == 0, sh0_ref[0],
                          zprev_ref[0, SUBLANES - 1:SUBLANES, 0:RWKV_SHIFT_W])
        row = lax.broadcasted_iota(jnp.int32, (ts, 1), 0)
        prev = jnp.where(row == 0, first, pltpu.roll(z, 1, axis=0))
    else:
        prev = sh0_ref[0]
    zm = z + (prev - z) * mu_ref[...]
    r = zm[:, 0:MIX_W]
    k = zm[:, MIX_W:2 * MIX_W]
    v = zm[:, 2 * MIX_W:3 * MIX_W]
    lowrank = zm[:, 3 * MIX_W:RWKV_SHIFT_W]
    u_w = _dot(jnp.tanh(lowrank).astype(BF16), wup_ref[...])
    u_a = _dot(lowrank.astype(BF16), aup_ref[...])
    w_log = -_softplus(-(w0_ref[...] + u_w)) - 0.5
    decay = jnp.exp(-jnp.exp(w_log))
    a = _sigmoid(a0_ref[...] + u_a)
    kn = k * kk_ref[...]
    ss = _dot3(kn * kn, ones_ref[...])
    kn = kn / jnp.maximum(jnp.sqrt(ss), 1e-12)
    k2 = k * (1.0 + (a - 1.0) * ka_ref[...])
    bonus = _dot3(r * k2 * rk_ref[...], ones_ref[...]) * v
    r_out[0] = r
    k_out[0] = k2
    v_out[0] = v
    w_out[0] = decay
    kn_out[0] = kn
    b_out[0] = kn * a
    bonus_out[0] = bonus


def _rwkv_prep(z3, shift, p, ts, seq):
    b, s, _ = z3.shape
    row = lambda i, j: (0, 0)
    vec = lambda n: pl.BlockSpec((1, n), row)
    out_sds = jax.ShapeDtypeStruct((b, s, MIX_W), F32)
    out_spec = pl.BlockSpec((1, ts, MIX_W), lambda i, j: (i, j, 0))
    if seq:
        shift_spec = pl.BlockSpec((1, 1, RWKV_SHIFT_W), lambda i, j: (i, 0, 0))
    else:
        shift_spec = pl.BlockSpec((1, ts, RWKV_SHIFT_W), lambda i, j: (i, j, 0))
    return pl.pallas_call(
        functools.partial(_rwkv_prep_kernel, ts=ts, seq=seq),
        out_shape=[out_sds] * 7,
        grid=(b, s // ts),
        in_specs=[
            pl.BlockSpec((1, ts, RW_BLOCK_W), lambda i, j: (i, j, 0)),
            pl.BlockSpec((1, SUBLANES, RW_BLOCK_W),
                         lambda i, j: (i, jnp.maximum(j * (ts // SUBLANES) - 1, 0), 0)),
            shift_spec,
            vec(RWKV_SHIFT_W), vec(MIX_W), vec(MIX_W), vec(MIX_W), vec(MIX_W), vec(MIX_W),
            pl.BlockSpec((LANES, MIX_W), row),
            pl.BlockSpec((LANES, MIX_W), row),
            pl.BlockSpec((MIX_W, MIX_W), row),
        ],
        out_specs=[out_spec] * 7,
        compiler_params=_cparams(("parallel", "parallel")),
        name="rwkv_prep",
    )(z3, z3, shift, p["mu"], p["w0"], p["a0"], p["k_k"], p["k_a"], p["r_k"],
      p["w_up"], p["a_up"], p["ones"])


def _rwkv_scan_kernel(r_ref, k_ref, v_ref, w_ref, kn_ref, b_ref, y_ref, sfin_ref, s_ref, *, tb):
    @pl.when(pl.program_id(0) == 0)
    def _():
        s_ref[...] = jnp.zeros_like(s_ref)

    rl = s_ref.shape[1]

    def row(ref, t, j):
        return jnp.broadcast_to(ref[t, j:j + 1, :], (rl, LANES))

    def token(t, carry):
        parts = [jnp.zeros((rl, LANES), F32) for _ in range(4)]
        for j in range(HEAD_DIM):
            parts[j % 4] = parts[j % 4] + s_ref[j] * row(kn_ref, t, j)
        sa = -((parts[0] + parts[1]) + (parts[2] + parts[3]))
        vt = v_ref[t]
        yparts = [jnp.zeros((rl, LANES), F32) for _ in range(4)]
        for j in range(HEAD_DIM):
            sj = s_ref[j] * row(w_ref, t, j) + sa * row(b_ref, t, j) + vt * row(k_ref, t, j)
            s_ref[j] = sj
            yparts[j % 4] = yparts[j % 4] + sj * row(r_ref, t, j)
        y_ref[t] = (yparts[0] + yparts[1]) + (yparts[2] + yparts[3])
        return carry

    lax.fori_loop(0, tb, token, 0)

    @pl.when(pl.program_id(0) == pl.num_programs(0) - 1)
    def _():
        sfin_ref[...] = s_ref[...]


def _rwkv_scan(rs, ks, vs, ws, kns, bs, tb):
    s = rs.shape[0]
    rl = vs.shape[1]
    jspec = pl.BlockSpec((tb, HEAD_DIM, LANES), lambda i: (i, 0, 0))
    ispec = pl.BlockSpec((tb, rl, LANES), lambda i: (i, 0, 0))
    return pl.pallas_call(
        functools.partial(_rwkv_scan_kernel, tb=tb),
        out_shape=[jax.ShapeDtypeStruct((s, rl, LANES), F32),
                   jax.ShapeDtypeStruct((HEAD_DIM, rl, LANES), F32)],
        grid=(s // tb,),
        in_specs=[jspec, jspec, ispec, jspec, jspec, jspec],
        out_specs=[ispec, pl.BlockSpec((HEAD_DIM, rl, LANES), lambda i: (0, 0, 0))],
        scratch_shapes=[pltpu.VMEM((HEAD_DIM, rl, LANES), F32)],
        compiler_params=_cparams(("arbitrary",)),
        name="rwkv_scan",
    )(rs, ks, vs, ws, kns, bs)


def _to_scan_j(x, ip):
    b, s, _ = x.shape
    t = x.reshape(b, s, N_HEADS, HEAD_DIM).transpose(1, 3, 0, 2).reshape(s, HEAD_DIM, b * N_HEADS)
    return jnp.concatenate([t] * ip, axis=-1)


def _to_scan_i(x, ip):
    b, s, _ = x.shape
    rl = HEAD_DIM // ip
    t = x.reshape(b, s, N_HEADS, ip, rl).transpose(1, 4, 3, 0, 2)
    return t.reshape(s, rl, ip * b * N_HEADS)


def _from_scan_i(y, b, ip):
    s, rl, _ = y.shape
    t = y.reshape(s, rl, ip, b, N_HEADS).transpose(3, 0, 4, 2, 1)
    return t.reshape(b, s, MIX_W)


def _state_from_scan(sf, b, ip):
    rl = sf.shape[1]
    t = sf.reshape(HEAD_DIM, rl, ip, b, N_HEADS).transpose(3, 4, 2, 1, 0)
    return t.reshape(b, N_HEADS, HEAD_DIM, HEAD_DIM)


def _rwkv_step_kernel(s_ref, r_ref, k_ref, v_ref, w_ref, kn_ref, b_ref, snew_ref, y_ref):
    for h in range(N_HEADS):
        s = s_ref[0, h]
        sa = -jnp.sum(s * kn_ref[0, h], axis=-1, keepdims=True)
        sn = s * w_ref[0, h] + sa * b_ref[0, h] + v_ref[0, h] * k_ref[0, h]
        snew_ref[0, h] = sn
        y_ref[0, h] = jnp.sum(sn * r_ref[0, h], axis=-1, keepdims=True)


def _rwkv_step(state, r, k, v, w, kn, bb):
    b = state.shape[0]
    jrow = lambda x: x.reshape(b, N_HEADS, 1, HEAD_DIM)
    jspec = pl.BlockSpec((1, N_HEADS, 1, HEAD_DIM), lambda i: (i, 0, 0, 0))
    cspec = pl.BlockSpec((1, N_HEADS, HEAD_DIM, 1), lambda i: (i, 0, 0, 0))
    sspec = pl.BlockSpec((1, N_HEADS, HEAD_DIM, HEAD_DIM), lambda i: (i, 0, 0, 0))
    snew, y = pl.pallas_call(
        _rwkv_step_kernel,
        out_shape=[jax.ShapeDtypeStruct(state.shape, F32),
                   jax.ShapeDtypeStruct((b, N_HEADS, HEAD_DIM, 1), F32)],
        grid=(b,),
        in_specs=[sspec, jspec, jspec, cspec, jspec, jspec, jspec],
        out_specs=[sspec, cspec],
        compiler_params=_cparams(("parallel",)),
        name="rwkv_step",
    )(state, jrow(r), jrow(k), v.reshape(b, N_HEADS, HEAD_DIM, 1), jrow(w), jrow(kn), jrow(bb))
    return snew, y.reshape(b, 1, MIX_W)


def _rwkv_post_kernel(y_ref, bonus_ref, g_ref, lnw_ref, lnb_ref, ones_ref, o_ref):
    y = y_ref[0]
    inv = 1.0 / HEAD_DIM
    mean = _dot3(y, ones_ref[...]) * inv
    yc = y - mean
    var = _dot3(yc * yc, ones_ref[...]) * inv
    yn = yc * lax.rsqrt(var + GN_EPS) * lnw_ref[...] + lnb_ref[...]
    o_ref[0] = (yn + bonus_ref[0]) * _silu(g_ref[0])


def _rwkv_post(y, bonus, z3, p, ts):
    b, s, _ = y.shape
    blk = pl.BlockSpec((1, ts, MIX_W), lambda i, j: (i, j, 0))
    row = lambda i, j: (0, 0)
    return pl.pallas_call(
        _rwkv_post_kernel,
        out_shape=jax.ShapeDtypeStruct((b, s, MIX_W), F32),
        grid=(b, s // ts),
        in_specs=[blk, blk,
                  pl.BlockSpec((1, ts, MIX_W), lambda i, j: (i, j, C_GRW // MIX_W)),
                  pl.BlockSpec((1, MIX_W), row), pl.BlockSpec((1, MIX_W), row),
                  pl.BlockSpec((MIX_W, MIX_W), row)],
        out_specs=blk,
        compiler_params=_cparams(("parallel", "parallel")),
        name="rwkv_post",
    )(y, bonus, z3, p["ln_w"], p["ln_b"], p["ones"])


def _logf(x, fb):
    u = x + fb
    return jnp.minimum(u, 0.0) - jnp.log(1.0 + jnp.exp(-jnp.abs(u)))


def _logf_kernel(f_ref, fb_ref, tri_ref, lf_ref, cum_ref, cumt_ref, carry_ref):
    @pl.when(pl.program_id(1) == 0)
    def _():
        carry_ref[...] = jnp.zeros_like(carry_ref)

    lf = _logf(f_ref[0], fb_ref[...])
    lf_ref[0] = lf
    hi, mid, lo = _split3(lf)
    tri = tri_ref[...]
    cum = _dot(tri, hi) + _dot(tri, mid) + _dot(tri, lo) + carry_ref[...]
    cum_ref[0] = cum
    cumt_ref[0] = cum.T
    carry_ref[...] = cum[cum.shape[0] - 1:, :]


def _fox_logf(z3, fbias_row, ts):
    b, s, _ = z3.shape
    r = lax.broadcasted_iota(jnp.int32, (ts, ts), 0)
    c = lax.broadcasted_iota(jnp.int32, (ts, ts), 1)
    tri = (c <= r).astype(BF16)
    blk = pl.BlockSpec((1, ts, LANES), lambda i, j: (i, j, 0))
    return pl.pallas_call(
        _logf_kernel,
        out_shape=[jax.ShapeDtypeStruct((b, s, LANES), F32),
                   jax.ShapeDtypeStruct((b, s, LANES), F32),
                   jax.ShapeDtypeStruct((b, LANES, s), F32)],
        grid=(b, s // ts),
        in_specs=[pl.BlockSpec((1, ts, LANES), lambda i, j: (i, j, C_FXF // LANES)),
                  pl.BlockSpec((1, LANES), lambda i, j: (0, 0)),
                  pl.BlockSpec((ts, ts), lambda i, j: (0, 0))],
        out_specs=[blk, blk, pl.BlockSpec((1, LANES, ts), lambda i, j: (i, 0, j))],
        scratch_shapes=[pltpu.VMEM((1, LANES), F32)],
        compiler_params=_cparams(("parallel", "arbitrary")),
        name="fox_logf",
    )(z3, fbias_row, tri)


def _head_masked_q(q_ref, h):
    g = h // 2
    q = q_ref[0, :, g * LANES:(g + 1) * LANES] * ATTN_SCALE
    if q.shape[0] < SUBLANES:
        q = jnp.broadcast_to(q[0:1], (SUBLANES, LANES))
    lane = lax.broadcasted_iota(jnp.int32, q.shape, 1)
    keep = (lane >= HEAD_DIM) if (h % 2) else (lane < HEAD_DIM)
    return jnp.where(keep, q, 0.0).astype(BF16)


def _pair_select(lo_half, hi_half):
    lane = lax.broadcasted_iota(jnp.int32, lo_half.shape, 1)
    return jnp.where(lane < HEAD_DIM, lo_half, hi_half)


def _sb_prompt_kernel(q_ref, k_ref, v_ref, g_ref, tri_ref, o_ref, acc_ref, carry_ref, *, tq, tk):
    qi = pl.program_id(1)
    kj = pl.program_id(2)

    @pl.when(kj == 0)
    def _():
        acc_ref[...] = jnp.zeros_like(acc_ref)
        carry_ref[...] = jnp.zeros_like(carry_ref)

    @pl.when(kj <= qi)
    def _():
        kv = qi - kj
        rowp = qi * tq + lax.broadcasted_iota(jnp.int32, (tq, tk), 0)
        colp = kv * tk + lax.broadcasted_iota(jnp.int32, (tq, tk), 1)
        mask = colp < rowp
        kb = k_ref[0].astype(BF16)
        vb = v_ref[0].astype(BF16)
        tri = tri_ref[...]
        for g in range(N_HEADS // 2):
            outs = []
            for h in (2 * g, 2 * g + 1):
                z = _dot_nt(_head_masked_q(q_ref, h), kb[:, g * LANES:(g + 1) * LANES])
                sp = jnp.where(mask, _softplus(z), 0.0)
                hi = sp.astype(BF16)
                lo = (sp - hi.astype(F32)).astype(BF16)
                incl = _dot(hi, tri) + _dot(lo, tri)
                c = carry_ref[h]
                suffix = incl - sp + jnp.concatenate([c] * (tk // LANES), axis=1)
                a = jnp.where(mask, jnp.exp(z - sp - suffix), 0.0)
                outs.append(_dot(a.astype(BF16), vb[:, g * LANES:(g + 1) * LANES]))
                carry_ref[h] = c + jnp.broadcast_to(incl[:, 0:1], (tq, LANES))
            acc_ref[:, g * LANES:(g + 1) * LANES] += _pair_select(outs[0], outs[1])

    @pl.when(kj == pl.num_programs(2) - 1)
    def _():
        o_ref[0] = acc_ref[...] * _silu(g_ref[0])


def _sb_prompt(z3, tq, tk):
    b, s, _ = z3.shape
    r = lax.broadcasted_iota(jnp.int32, (tk, tk), 0)
    c = lax.broadcasted_iota(jnp.int32, (tk, tk), 1)
    tri = (r >= c).astype(BF16)
    qblk = lambda col: pl.BlockSpec((1, tq, MIX_W), lambda i, j, k: (i, j, col // MIX_W))
    kblk = lambda col: pl.BlockSpec(
        (1, tk, MIX_W), lambda i, j, k: (i, jnp.maximum(j - k, 0), col // MIX_W))
    return pl.pallas_call(
        functools.partial(_sb_prompt_kernel, tq=tq, tk=tk),
        out_shape=jax.ShapeDtypeStruct((b, s, MIX_W), F32),
        grid=(b, s // tq, s // tk),
        in_specs=[qblk(C_SBQ), kblk(C_SBK), kblk(C_SBV), qblk(C_GSB),
                  pl.BlockSpec((tk, tk), lambda i, j, k: (0, 0))],
        out_specs=pl.BlockSpec((1, tq, MIX_W), lambda i, j, k: (i, j, 0)),
        scratch_shapes=[pltpu.VMEM((tq, MIX_W), F32), pltpu.VMEM((N_HEADS, tq, LANES), F32)],
        compiler_params=_cparams(("parallel", "parallel", "arbitrary")),
        name="sb_prompt",
    )(z3, z3, z3, z3, tri)


def _fox_prompt_kernel(q_ref, k_ref, v_ref, g_ref, fq_ref, fk_ref, o_ref,
                       acc_ref, m_ref, l_ref, *, tq, tk):
    qi = pl.program_id(1)
    kj = pl.program_id(2)

    @pl.when(kj == 0)
    def _():
        acc_ref[...] = jnp.zeros_like(acc_ref)
        m_ref[...] = jnp.full_like(m_ref, -jnp.inf)
        l_ref[...] = jnp.zeros_like(l_ref)

    @pl.when(kj <= qi)
    def _():
        rowp = qi * tq + lax.broadcasted_iota(jnp.int32, (tq, tk), 0)
        colp = kj * tk + lax.broadcasted_iota(jnp.int32, (tq, tk), 1)
        mask = colp <= rowp
        kb = k_ref[0].astype(BF16)
        vb = v_ref[0].astype(BF16)
        fq = fq_ref[0]
        fk = fk_ref[0]
        for g in range(N_HEADS // 2):
            outs, alphas = [], []
            for h in (2 * g, 2 * g + 1):
                z = _dot_nt(_head_masked_q(q_ref, h), kb[:, g * LANES:(g + 1) * LANES])
                z = z + fq[:, h:h + 1] - fk[h:h + 1, :]
                z = jnp.where(mask, z, -jnp.inf)
                m_old = m_ref[h]
                m_new = jnp.maximum(m_old, jnp.max(z, axis=-1, keepdims=True))
                alpha = jnp.exp(m_old - m_new)
                p = jnp.exp(z - m_new)
                l_ref[h] = l_ref[h] * alpha + jnp.sum(p, axis=-1, keepdims=True)
                m_ref[h] = m_new
                outs.append(_dot(p.astype(BF16), vb[:, g * LANES:(g + 1) * LANES]))
                alphas.append(jnp.broadcast_to(alpha, (tq, LANES)))
            sl = slice(g * LANES, (g + 1) * LANES)
            acc_ref[:, sl] = (acc_ref[:, sl] * _pair_select(alphas[0], alphas[1])
                              + _pair_select(outs[0], outs[1]))

    @pl.when(kj == pl.num_programs(2) - 1)
    def _():
        for g in range(N_HEADS // 2):
            sl = slice(g * LANES, (g + 1) * LANES)
            inv = _pair_select(jnp.broadcast_to(1.0 / l_ref[2 * g], (tq, LANES)),
                               jnp.broadcast_to(1.0 / l_ref[2 * g + 1], (tq, LANES)))
            o_ref[0, :, sl] = acc_ref[:, sl] * inv * _silu(g_ref[0][:, sl])


def _fox_prompt(z3, cum, cumt, tq, tk):
    b, s, _ = z3.shape
    qblk = lambda col: pl.BlockSpec((1, tq, MIX_W), lambda i, j, k: (i, j, col // MIX_W))
    kblk = lambda col: pl.BlockSpec(
        (1, tk, MIX_W), lambda i, j, k: (i, jnp.minimum(k, j), col // MIX_W))
    return pl.pallas_call(
        functools.partial(_fox_prompt_kernel, tq=tq, tk=tk),
        out_shape=jax.ShapeDtypeStruct((b, s, MIX_W), F32),
        grid=(b, s // tq, s // tk),
        in_specs=[qblk(C_FXQ), kblk(C_FXK), kblk(C_FXV), qblk(C_GFX),
                  pl.BlockSpec((1, tq, LANES), lambda i, j, k: (i, j, 0)),
                  pl.BlockSpec((1, SUBLANES, tk), lambda i, j, k: (i, 0, jnp.minimum(k, j)))],
        out_specs=pl.BlockSpec((1, tq, MIX_W), lambda i, j, k: (i, j, 0)),
        scratch_shapes=[pltpu.VMEM((tq, MIX_W), F32),
                        pltpu.VMEM((N_HEADS, tq, 1), F32), pltpu.VMEM((N_HEADS, tq, 1), F32)],
        compiler_params=_cparams(("parallel", "parallel", "arbitrary")),
        name="fox_prompt",
    )(z3, z3, z3, z3, cum, cumt)


def _mem_kernel(q_ref, g_ref, mk_ref, mv_ref, o_ref):
    kb = mk_ref[0].astype(BF16)
    vb = mv_ref[0].astype(BF16)
    for g in range(MEM_HEADS // 2):
        outs = []
        for h in (2 * g, 2 * g + 1):
            z = _dot_nt(_head_masked_q(q_ref, h), kb[:, g * LANES:(g + 1) * LANES])
            m = jnp.max(z, axis=-1, keepdims=True)
            e = jnp.exp(z - m)
            p = e / jnp.sum(e, axis=-1, keepdims=True)
            outs.append(_dot(p.astype(BF16), vb[:, g * LANES:(g + 1) * LANES]))
        sl = slice(g * LANES, (g + 1) * LANES)
        o = _pair_select(outs[0], outs[1])
        o_ref[0, :, sl] = o[:o_ref.shape[1]] * _silu(g_ref[0, :, sl])


def _mem_attend(z3, mk, mv, tq):
    b, s, _ = z3.shape
    n_mem = mk.shape[1]
    kv = pl.BlockSpec((1, n_mem, MEM_W), lambda i, j: (i, 0, 0))
    return pl.pallas_call(
        _mem_kernel,
        out_shape=jax.ShapeDtypeStruct((b, s, MEM_W), F32),
        grid=(b, s // tq),
        in_specs=[pl.BlockSpec((1, tq, MEM_W), lambda i, j: (i, j, C_MQ // MEM_W)),
                  pl.BlockSpec((1, tq, MEM_W), lambda i, j: (i, j, C_GM // MEM_W)),
                  kv, kv],
        out_specs=pl.BlockSpec((1, tq, MEM_W), lambda i, j: (i, j, 0)),
        compiler_params=_cparams(("parallel", "parallel")),
        name="mem_attend",
    )(z3, z3, mk, mv)


def _merge_kernel(x_ref, yrw_ref, ysb_ref, yfx_ref, ym_ref, g0_ref, g1_ref, g2_ref, g3_ref,
                  wrw_ref, wsb_ref, wfx_ref, wm_ref, wout_ref, fg_ref, xo_ref, yo_ref):
    def branch(y_ref, w_ref, gate_ref):
        return _sigmoid(gate_ref[0]) * _dot(y_ref[0].astype(BF16), w_ref[...])

    merged = (branch(yrw_ref, wrw_ref, g0_ref) + branch(ysb_ref, wsb_ref, g1_ref)
              + branch(yfx_ref, wfx_ref, g2_ref) + branch(ym_ref, wm_ref, g3_ref))
    xn = x_ref[0] + _dot(merged.astype(BF16), wout_ref[...])
    xo_ref[0] = xn
    ms = jnp.mean(xn * xn, axis=-1, keepdims=True)
    yo_ref[0] = xn * lax.rsqrt(ms + NORM_EPS) * fg_ref[...]


def _merge(x3, yrw, ysb, yfx, ym, z3, wts, final_g, tm):
    b, s, d = x3.shape
    row = lambda i, j: (0, 0)
    blk = lambda w: pl.BlockSpec((1, tm, w), lambda i, j: (i, j, 0))
    gate = lambda n: pl.BlockSpec((1, tm, d), lambda i, j: (i, j, C_GATE // d + n))
    full = lambda a: pl.BlockSpec(a.shape, row)
    return pl.pallas_call(
        _merge_kernel,
        out_shape=[jax.ShapeDtypeStruct(x3.shape, F32)] * 2,
        grid=(b, s // tm),
        in_specs=[blk(d), blk(MIX_W), blk(MIX_W), blk(MIX_W), blk(MEM_W),
                  gate(0), gate(1), gate(2), gate(3),
                  full(wts["rw"]), full(wts["sb"]), full(wts["fx"]), full(wts["m"]),
                  full(wts["out"]), pl.BlockSpec((1, d), row)],
        out_specs=[blk(d), blk(d)],
        compiler_params=_cparams(("parallel", "parallel")),
        name="merge",
    )(x3, yrw, ysb, yfx, ym, z3, z3, z3, z3,
      wts["rw"], wts["sb"], wts["fx"], wts["m"], wts["out"], final_g.reshape(1, d))


def _block_diag_q(q_row):
    q = jnp.broadcast_to(q_row * ATTN_SCALE, (N_HEADS, MIX_W))
    return jnp.where(_head_rows(), q, 0.0).astype(BF16)


def _head_rows():
    r = lax.broadcasted_iota(jnp.int32, (N_HEADS, MIX_W), 0)
    c = lax.broadcasted_iota(jnp.int32, (N_HEADS, MIX_W), 1) // HEAD_DIM
    return r == c


def _collapse_heads(acc):
    return jnp.sum(jnp.where(_head_rows(), acc, 0.0), axis=0, keepdims=True)


def _sb_decode_kernel(pt_ref, q_ref, kn_ref, vn_ref, g_ref, kc_ref, vc_ref, tri_ref, o_ref,
                      qbd_ref, acc_ref, carry_ref, *, page):
    p = pl.program_id(1)
    n_pages = pl.num_programs(1)

    @pl.when(p == 0)
    def _():
        qbd = _block_diag_q(q_ref[0])
        qbd_ref[...] = qbd
        new_pos = n_pages * page
        new_mask = new_pos < new_pos
        z_new = jnp.sum(qbd.astype(F32) * kn_ref[0].astype(BF16).astype(F32), axis=-1, keepdims=True)
        sp_new = jnp.where(new_mask, _softplus(z_new), 0.0)
        a_new = jnp.where(new_mask, jnp.exp(z_new - sp_new), 0.0)
        acc_ref[...] = a_new.astype(BF16).astype(F32) * vn_ref[0].astype(BF16).astype(F32)
        carry_ref[...] = jnp.broadcast_to(sp_new, carry_ref.shape)

    kb = kc_ref[...].astype(BF16)
    vb = vc_ref[...].astype(BF16)
    z = _dot_nt(qbd_ref[...], kb)
    sp = _softplus(z)
    hi = sp.astype(BF16)
    lo = (sp - hi.astype(F32)).astype(BF16)
    both = _dot(jnp.concatenate([hi, lo], axis=1), tri_ref[...])
    suffix = both[:, :page] + carry_ref[...]
    a = jnp.exp(z - sp - suffix)
    acc_ref[...] += _dot(a.astype(BF16), vb)
    carry_ref[...] += both[:, page:]

    @pl.when(p == n_pages - 1)
    def _():
        o_ref[0] = _collapse_heads(acc_ref[...]) * _silu(g_ref[0])


def _suffix_total_matrix(page):
    r = lax.broadcasted_iota(jnp.int32, (2 * page, 2 * page), 0) % page
    c = lax.broadcasted_iota(jnp.int32, (2 * page, 2 * page), 1)
    return jnp.where(c < page, r > c, True).astype(BF16)


def _sb_decode(zs3, cache_k, cache_v, page_table, layer):
    b = zs3.shape[0]
    n_pages = page_table.shape[1]
    page = cache_k.shape[2]
    kc = cache_k.reshape(cache_k.shape[0], cache_k.shape[1], page, MIX_W)
    vc = cache_v.reshape(kc.shape)
    row = lambda col: pl.BlockSpec((1, 1, MIX_W), lambda i, p, pt: (i, 0, col // MIX_W))
    pool = pl.BlockSpec((None, None, page, MIX_W),
                        lambda i, p, pt: (layer, pt[i, n_pages - 1 - p], 0, 0))
    grid_spec = pltpu.PrefetchScalarGridSpec(
        num_scalar_prefetch=1,
        grid=(b, n_pages),
        in_specs=[row(C_SBQ), row(C_SBK), row(C_SBV), row(C_GSB), pool, pool,
                  pl.BlockSpec((2 * page, 2 * page), lambda i, p, pt: (0, 0))],
        out_specs=pl.BlockSpec((1, 1, MIX_W), lambda i, p, pt: (i, 0, 0)),
        scratch_shapes=[pltpu.VMEM((N_HEADS, MIX_W), BF16), pltpu.VMEM((N_HEADS, MIX_W), F32),
                        pltpu.VMEM((N_HEADS, page), F32)],
    )
    return pl.pallas_call(
        functools.partial(_sb_decode_kernel, page=page),
        out_shape=jax.ShapeDtypeStruct((b, 1, MIX_W), F32),
        grid_spec=grid_spec,
        compiler_params=_cparams(("parallel", "arbitrary")),
        name="sb_decode",
    )(page_table, zs3, zs3, zs3, zs3, kc, vc, _suffix_total_matrix(page))


def _fox_decode_kernel(pt_ref, q_ref, kn_ref, vn_ref, g_ref, f_ref, fb_ref, kc_ref, vc_ref, lc_ref,
                       tri_ref, o_ref, lf_ref, qbd_ref, acc_ref, carry_ref, m_ref, l_ref, *, page):
    p = pl.program_id(1)
    n_pages = pl.num_programs(1)

    @pl.when(p == 0)
    def _():
        qbd = _block_diag_q(q_ref[0])
        qbd_ref[...] = qbd
        lf_new = _logf(f_ref[0], fb_ref[...])
        lf_ref[0] = lf_new
        r = lax.broadcasted_iota(jnp.int32, (N_HEADS, LANES), 0)
        c = lax.broadcasted_iota(jnp.int32, (N_HEADS, LANES), 1)
        lf_col = jnp.sum(jnp.where(r == c, jnp.broadcast_to(lf_new, (N_HEADS, LANES)), 0.0),
                         axis=-1, keepdims=True)
        carry_ref[...] = jnp.broadcast_to(lf_col, carry_ref.shape)
        z_new = jnp.sum(qbd.astype(F32) * kn_ref[0].astype(BF16).astype(F32), axis=-1, keepdims=True)
        m_ref[...] = z_new
        l_ref[...] = jnp.ones_like(l_ref)
        acc_ref[...] = jnp.broadcast_to(vn_ref[0].astype(BF16).astype(F32), acc_ref.shape)

    kb = kc_ref[...].astype(BF16)
    vb = vc_ref[...].astype(BF16)
    tri = tri_ref[...]
    eye = (lax.broadcasted_iota(jnp.int32, (N_HEADS, N_HEADS), 0)
           == lax.broadcasted_iota(jnp.int32, (N_HEADS, N_HEADS), 1)).astype(BF16)
    both = jnp.zeros((N_HEADS, 2 * page), F32)
    for piece in _split3(lc_ref[...]):
        both = both + _dot(_dot_nt(eye, piece).astype(BF16), tri)
    z = _dot_nt(qbd_ref[...], kb) + both[:, :page] + carry_ref[...]
    m_old = m_ref[...]
    m_new = jnp.maximum(m_old, jnp.max(z, axis=-1, keepdims=True))
    alpha = jnp.exp(m_old - m_new)
    pr = jnp.exp(z - m_new)
    l_ref[...] = l_ref[...] * alpha + jnp.sum(pr, axis=-1, keepdims=True)
    m_ref[...] = m_new
    acc_ref[...] = acc_ref[...] * alpha + _dot(pr.astype(BF16), vb)
    carry_ref[...] += both[:, page:]

    @pl.when(p == n_pages - 1)
    def _():
        o_ref[0] = _collapse_heads(acc_ref[...] / l_ref[...]) * _silu(g_ref[0])


def _fox_decode(zs3, cache_k, cache_v, cache_logf, fbias_row, page_table, layer):
    b = zs3.shape[0]
    n_pages = page_table.shape[1]
    page = cache_k.shape[2]
    kc = cache_k.reshape(cache_k.shape[0], cache_k.shape[1], page, MIX_W)
    vc = cache_v.reshape(kc.shape)
    r = lax.broadcasted_iota(jnp.int32, (page, 2 * page), 0)
    c = lax.broadcasted_iota(jnp.int32, (page, 2 * page), 1)
    tri = jnp.where(c < page, r > c, True).astype(BF16)
    row = lambda col: pl.BlockSpec((1, 1, MIX_W), lambda i, p, pt: (i, 0, col // MIX_W))
    pool = pl.BlockSpec((None, None, page, MIX_W),
                        lambda i, p, pt: (layer, pt[i, n_pages - 1 - p], 0, 0))
    grid_spec = pltpu.PrefetchScalarGridSpec(
        num_scalar_prefetch=1,
        grid=(b, n_pages),
        in_specs=[row(C_FXQ), row(C_FXK), row(C_FXV), row(C_GFX),
                  pl.BlockSpec((1, 1, LANES), lambda i, p, pt: (i, 0, C_FXF // LANES)),
                  pl.BlockSpec((1, LANES), lambda i, p, pt: (0, 0)),
                  pool, pool,
                  pl.BlockSpec((None, None, page, N_HEADS),
                               lambda i, p, pt: (layer, pt[i, n_pages - 1 - p], 0, 0)),
                  pl.BlockSpec((page, 2 * page), lambda i, p, pt: (0, 0))],
        out_specs=[pl.BlockSpec((1, 1, MIX_W), lambda i, p, pt: (i, 0, 0)),
                   pl.BlockSpec((1, 1, LANES), lambda i, p, pt: (i, 0, 0))],
        scratch_shapes=[pltpu.VMEM((N_HEADS, MIX_W), BF16), pltpu.VMEM((N_HEADS, MIX_W), F32),
                        pltpu.VMEM((N_HEADS, page), F32),
                        pltpu.VMEM((N_HEADS, 1), F32), pltpu.VMEM((N_HEADS, 1), F32)],
    )
    return pl.pallas_call(
        functools.partial(_fox_decode_kernel, page=page),
        out_shape=[jax.ShapeDtypeStruct((b, 1, MIX_W), F32),
                   jax.ShapeDtypeStruct((b, 1, LANES), F32)],
        grid_spec=grid_spec,
        compiler_params=_cparams(("parallel", "arbitrary")),
        name="fox_decode",
    )(page_table, zs3, zs3, zs3, zs3, zs3, fbias_row, kc, vc, cache_logf, tri)


def _reorder_w_in(w):
    c_grw = RWKV_SHIFT_W
    c_fxf = c_grw + 9 * MIX_W
    c_mq = c_fxf + N_HEADS
    pad = jnp.zeros((w.shape[0], RW_BLOCK_W - RWKV_SHIFT_W - N_HEADS), w.dtype)
    return jnp.concatenate(
        [w[:, :RWKV_SHIFT_W], w[:, c_fxf:c_mq], pad, w[:, c_grw:c_fxf], w[:, c_mq:]],
        axis=1).astype(BF16)


def _pick(n, pref):
    return pref if n % pref == 0 else n


def _heads(x, b, s):
    return x.reshape(b, s, -1, HEAD_DIM)


def kernel(x_prompt, x_sample, cache_sb_k, cache_sb_v, cache_fox_k, cache_fox_v, cache_fox_logf, cache_mem_k, cache_mem_v, state_rwkv_wkv, state_rwkv_shift, page_table, mem_prompt, norm_g, w_in, rwkv_mu, rwkv_w0, rwkv_w_up, rwkv_a0, rwkv_a_up, rwkv_k_k, rwkv_k_a, rwkv_r_k, rwkv_ln_w, rwkv_ln_b, fox_fbias, mem_norm_g, w_mem_kv, w_rwkv_o, w_sb_o, w_fox_o, w_mem_o, w_out, final_norm_g):
    depth = w_in.shape[0]
    bp, sp, d = x_prompt.shape
    bs, ss, _ = x_sample.shape
    assert ss == 1, "the sample group carries one new token per sequence"
    n_mem = mem_prompt.shape[1]
    bh = bp * N_HEADS
    ip = LANES // bh
    ones = _head_ones(MIX_W)
    zero_shift = jnp.zeros((bp, 1, RWKV_SHIFT_W), F32)

    hp, hs = x_prompt, x_sample
    yp = ys = None
    p_new = [[] for _ in range(9)]
    s_new = [[] for _ in range(7)]
    for l in range(depth):
        w_l = _reorder_w_in(w_in[l])
        rp = {
            "mu": rwkv_mu[l].reshape(1, -1), "w0": rwkv_w0[l].reshape(1, -1),
            "a0": rwkv_a0[l].reshape(1, -1), "k_k": rwkv_k_k[l].reshape(1, -1),
            "k_a": rwkv_k_a[l].reshape(1, -1), "r_k": rwkv_r_k[l].reshape(1, -1),
            "ln_w": rwkv_ln_w[l].reshape(1, -1), "ln_b": rwkv_ln_b[l].reshape(1, -1),
            "w_up": jnp.concatenate([rwkv_w_up[l], jnp.zeros_like(rwkv_a_up[l])], 0).astype(BF16),
            "a_up": jnp.concatenate([jnp.zeros_like(rwkv_w_up[l]), rwkv_a_up[l]], 0).astype(BF16),
            "ones": ones,
        }
        wts = {"rw": w_rwkv_o[l].astype(BF16), "sb": w_sb_o[l].astype(BF16),
               "fx": w_fox_o[l].astype(BF16), "m": w_mem_o[l].astype(BF16),
               "out": w_out[l].astype(BF16)}
        fbias_row = jnp.zeros((1, LANES), F32).at[0, :N_HEADS].set(fox_fbias[l])

        kvm = _norm_proj(mem_prompt.reshape(bp * n_mem, d), mem_norm_g[l],
                         w_mem_kv[l].astype(BF16), 2 * MEM_W).reshape(bp, n_mem, 2 * MEM_W)
        mk, mv = kvm[..., :MEM_W], kvm[..., MEM_W:]
        z = _norm_proj(hp.reshape(bp * sp, d), norm_g[l], w_l, 1024).reshape(bp, sp, N_PROJ)
        ts = _pick(sp, 256)
        r_, k_, v_, w_, kn_, b_, bonus = _rwkv_prep(z, zero_shift, rp, ts, True)
        y_s, s_fin = _rwkv_scan(_to_scan_j(r_, ip), _to_scan_j(k_, ip), _to_scan_i(v_, ip),
                                _to_scan_j(w_, ip), _to_scan_j(kn_, ip), _to_scan_j(b_, ip),
                                _pick(sp, 32))
        y_rw = _rwkv_post(_from_scan_i(y_s, bp, ip), bonus, z, rp, ts)
        y_sb = _sb_prompt(z, ts, ts)
        logf, cum, cumt = _fox_logf(z, fbias_row, ts)
        y_fx = _fox_prompt(z, cum, cumt, ts, ts)
        y_m = _mem_attend(z, mk, mv, ts)
        hp, yp = _merge(hp, y_rw, y_sb, y_fx, y_m, z, wts, final_norm_g, ts)
        st = (_heads(z[..., C_SBK:C_SBK + MIX_W], bp, sp), _heads(z[..., C_SBV:C_SBV + MIX_W], bp, sp),
              _heads(z[..., C_FXK:C_FXK + MIX_W], bp, sp), _heads(z[..., C_FXV:C_FXV + MIX_W], bp, sp),
              logf[..., :N_HEADS], _state_from_scan(s_fin, bp, ip), z[:, -1, :RWKV_SHIFT_W],
              _heads(mk, bp, n_mem), _heads(mv, bp, n_mem))
        for lst, arr in zip(p_new, st):
            lst.append(arr)

        zs = _norm_proj(hs.reshape(bs, d), norm_g[l], w_l, 1024)
        zs_rows = zs.reshape(1, bs, N_PROJ)
        zs_seq = zs.reshape(bs, 1, N_PROJ)
        r_, k_, v_, w_, kn_, b_, bonus = _rwkv_prep(
            zs_rows, state_rwkv_shift[l].reshape(1, bs, RWKV_SHIFT_W), rp, bs, False)
        s_next, y_s = _rwkv_step(state_rwkv_wkv[l], r_, k_, v_, w_, kn_, b_)
        y_rw = _rwkv_post(y_s.reshape(1, bs, MIX_W), bonus, zs_rows, rp, bs)
        y_sb = _sb_decode(zs_seq, cache_sb_k, cache_sb_v, page_table, l)
        y_fx, logf_s = _fox_decode(zs_seq, cache_fox_k, cache_fox_v, cache_fox_logf, fbias_row,
                                   page_table, l)
        y_m = _mem_attend(zs_seq, cache_mem_k[l].reshape(bs, n_mem, MEM_W),
                          cache_mem_v[l].reshape(bs, n_mem, MEM_W), 1)
        hs_rows, ys_rows = _merge(hs.reshape(1, bs, d), y_rw, y_sb.reshape(1, bs, MIX_W),
                                  y_fx.reshape(1, bs, MIX_W), y_m.reshape(1, bs, MEM_W),
                                  zs_rows, wts, final_norm_g, bs)
        hs, ys = hs_rows.reshape(bs, 1, d), ys_rows.reshape(bs, 1, d)
        st = (_heads(zs_seq[..., C_SBK:C_SBK + MIX_W], bs, 1), _heads(zs_seq[..., C_SBV:C_SBV + MIX_W], bs, 1),
              _heads(zs_seq[..., C_FXK:C_FXK + MIX_W], bs, 1), _heads(zs_seq[..., C_FXV:C_FXV + MIX_W], bs, 1),
              logf_s[..., :N_HEADS], s_next, zs[:, :RWKV_SHIFT_W])
        for lst, arr in zip(s_new, st):
            lst.append(arr)

    return (yp, ys) + tuple(jnp.stack(t) for t in p_new) + tuple(jnp.stack(t) for t in s_new)
```

```python
import functools

import jax
import jax.numpy as jnp
from jax import lax
from jax.experimental import pallas as pl
from jax.experimental.pallas import tpu as pltpu

F32 = jnp.float32
BF16 = jnp.bfloat16

D_MODEL = 1024
HEAD_DIM = 64
N_HEADS = 8
MIX_W = N_HEADS * HEAD_DIM
MEM_HEADS = 4
MEM_W = MEM_HEADS * HEAD_DIM
DECAY_RANK = 64
ICLR_RANK = 64
N_BRANCH = 4
NORM_EPS = 1e-6
GN_EPS = 64e-5
ATTN_SCALE = HEAD_DIM ** -0.5
RWKV_SHIFT_W = 3 * MIX_W + DECAY_RANK + ICLR_RANK

LANES = 128
SUBLANES = 8
VMEM_LIMIT = 48 * 1024 * 1024

RW_BLOCK_W = 2048
C_RW = 0
C_FXF = RWKV_SHIFT_W
C_GRW = 2048
C_SBQ, C_SBK, C_SBV, C_GSB = 2560, 3072, 3584, 4096
C_FXQ, C_FXK, C_FXV, C_GFX = 4608, 5120, 5632, 6144
C_MQ, C_GM = 6656, 6912
C_GATE = 7168
N_PROJ = C_GATE + N_BRANCH * D_MODEL


def _cparams(sem):
    return pltpu.CompilerParams(dimension_semantics=sem, vmem_limit_bytes=VMEM_LIMIT)


def _split3(x):
    hi = x.astype(BF16)
    r1 = x - hi.astype(F32)
    mid = r1.astype(BF16)
    lo = (r1 - mid.astype(F32)).astype(BF16)
    return hi, mid, lo


def _dot(a, b):
    return jnp.dot(a, b, preferred_element_type=F32)


def _dot_nt(a, b):
    return lax.dot_general(a, b, (((1,), (1,)), ((), ())), preferred_element_type=F32)


def _dot3(x, ones):
    hi, mid, lo = _split3(x)
    return _dot(hi, ones) + _dot(mid, ones) + _dot(lo, ones)


def _softplus(x):
    return jnp.maximum(x, 0.0) + jnp.log(1.0 + jnp.exp(-jnp.abs(x)))


def _sigmoid(x):
    return 1.0 / (1.0 + jnp.exp(-x))


def _silu(x):
    return x * _sigmoid(x)


def _head_ones(width):
    r = lax.broadcasted_iota(jnp.int32, (width, width), 0) // HEAD_DIM
    c = lax.broadcasted_iota(jnp.int32, (width, width), 1) // HEAD_DIM
    return (r == c).astype(BF16)


def _proj_kernel(x_ref, g_ref, w_ref, o_ref, xn_ref):
    @pl.when(pl.program_id(1) == 0)
    def _():
        x = x_ref[...]
        ms = jnp.mean(x * x, axis=-1, keepdims=True)
        xn_ref[...] = (x * lax.rsqrt(ms + NORM_EPS) * g_ref[...]).astype(BF16)

    o_ref[...] = _dot(xn_ref[...], w_ref[...])


def _norm_proj(x2d, g, w_bf16, tn):
    m, d = x2d.shape
    n = w_bf16.shape[1]
    tm = next((c for c in (1024, 512, 256) if m % c == 0), m)
    assert n % tn == 0
    return pl.pallas_call(
        _proj_kernel,
        out_shape=jax.ShapeDtypeStruct((m, n), F32),
        grid=(m // tm, n // tn),
        in_specs=[
            pl.BlockSpec((tm, d), lambda i, j: (i, 0)),
            pl.BlockSpec((1, d), lambda i, j: (0, 0)),
            pl.BlockSpec((d, tn), lambda i, j: (0, j)),
        ],
        out_specs=pl.BlockSpec((tm, tn), lambda i, j: (i, j)),
        scratch_shapes=[pltpu.VMEM((tm, d), BF16)],
        compiler_params=_cparams(("parallel", "arbitrary")),
        name="norm_proj",
    )(x2d, g.reshape(1, d), w_bf16)


def _rwkv_prep_kernel(z_ref, zprev_ref, sh0_ref, mu_ref, w0_ref, a0_ref, kk_ref, ka_ref, rk_ref,
                      wup_ref, aup_ref, ones_ref,
                      r_out, k_out, v_out, w_out, kn_out, b_out, bonus_out, *, ts, seq):
    z = z_ref[0, :, 0:RWKV_SHIFT_W]
    if seq:
        first = jnp.where(pl.program_id(1) == 0, sh0_ref[0],
                          zprev_ref[0, SUBLANES - 1:SUBLANES, 0:RWKV_SHIFT_W])
        row = lax.broadcasted_iota(jnp.int32, (ts, 1), 0)
        prev = jnp.where(row == 0, first, pltpu.roll(z, 1, axis=0))
    else:
        prev = sh0_ref[0]
    zm = z + (prev - z) * mu_ref[...]
    r = zm[:, 0:MIX_W]
    k = zm[:, MIX_W:2 * MIX_W]
    v = zm[:, 2 * MIX_W:3 * MIX_W]
    lowrank = zm[:, 3 * MIX_W:RWKV_SHIFT_W]
    u_w = _dot(jnp.tanh(lowrank).astype(BF16), wup_ref[...])
    u_a = _dot(lowrank.astype(BF16), aup_ref[...])
    w_log = -_softplus(-(w0_ref[...] + u_w)) - 0.5
    decay = jnp.exp(-jnp.exp(w_log))
    a = _sigmoid(a0_ref[...] + u_a)
    kn = k * kk_ref[...]
    ss = _dot3(kn * kn, ones_ref[...])
    kn = kn / jnp.maximum(jnp.sqrt(ss), 1e-12)
    k2 = k * (1.0 + (a - 1.0) * ka_ref[...])
    bonus = _dot3(r * k2 * rk_ref[...], ones_ref[...]) * v
    r_out[0] = r
    k_out[0] = k2
    v_out[0] = v
    w_out[0] = decay
    kn_out[0] = kn
    b_out[0] = kn * a
    bonus_out[0] = bonus


def _rwkv_prep(z3, shift, p, ts, seq):
    b, s, _ = z3.shape
    row = lambda i, j: (0, 0)
    vec = lambda n: pl.BlockSpec((1, n), row)
    out_sds = jax.ShapeDtypeStruct((b, s, MIX_W), F32)
    out_spec = pl.BlockSpec((1, ts, MIX_W), lambda i, j: (i, j, 0))
    if seq:
        shift_spec = pl.BlockSpec((1, 1, RWKV_SHIFT_W), lambda i, j: (i, 0, 0))
    else:
        shift_spec = pl.BlockSpec((1, ts, RWKV_SHIFT_W), lambda i, j: (i, j, 0))
    return pl.pallas_call(
        functools.partial(_rwkv_prep_kernel, ts=ts, seq=seq),
        out_shape=[out_sds] * 7,
        grid=(b, s // ts),
        in_specs=[
            pl.BlockSpec((1, ts, RW_BLOCK_W), lambda i, j: (i, j, 0)),
            pl.BlockSpec((1, SUBLANES, RW_BLOCK_W),
                         lambda i, j: (i, jnp.maximum(j * (ts // SUBLANES) - 1, 0), 0)),
            shift_spec,
            vec(RWKV_SHIFT_W), vec(MIX_W), vec(MIX_W), vec(MIX_W), vec(MIX_W), vec(MIX_W),
            pl.BlockSpec((LANES, MIX_W), row),
            pl.BlockSpec((LANES, MIX_W), row),
            pl.BlockSpec((MIX_W, MIX_W), row),
        ],
        out_specs=[out_spec] * 7,
        compiler_params=_cparams(("parallel", "parallel")),
        name="rwkv_prep",
    )(z3, z3, shift, p["mu"], p["w0"], p["a0"], p["k_k"], p["k_a"], p["r_k"],
      p["w_up"], p["a_up"], p["ones"])


def _rwkv_scan_kernel(r_ref, k_ref, v_ref, w_ref, kn_ref, b_ref, y_ref, sfin_ref, s_ref, *, tb):
    @pl.when(pl.program_id(0) == 0)
    def _():
        s_ref[...] = jnp.zeros_like(s_ref)

    rl = s_ref.shape[1]

    def row(ref, t, j):
        return jnp.broadcast_to(ref[t, j:j + 1, :], (rl, LANES))

    def token(t, carry):
        parts = [jnp.zeros((rl, LANES), F32) for _ in range(4)]
        for j in range(HEAD_DIM):
            parts[j % 4] = parts[j % 4] + s_ref[j] * row(kn_ref, t, j)
        sa = -((parts[0] + parts[1]) + (parts[2] + parts[3]))
        vt = v_ref[t]
        yparts = [jnp.zeros((rl, LANES), F32) for _ in range(4)]
        for j in range(HEAD_DIM):
            sj = s_ref[j] * row(w_ref, t, j) + sa * row(b_ref, t, j) + vt * row(k_ref, t, j)
            s_ref[j] = sj
            yparts[j % 4] = yparts[j % 4] + sj * row(r_ref, t, j)
        y_ref[t] = (yparts[0] + yparts[1]) + (yparts[2] + yparts[3])
        return carry

    lax.fori_loop(0, tb, token, 0)

    @pl.when(pl.program_id(0) == pl.num_programs(0) - 1)
    def _():
        sfin_ref[...] = s_ref[...]


def _rwkv_scan(rs, ks, vs, ws, kns, bs, tb):
    s = rs.shape[0]
    rl = vs.shape[1]
    jspec = pl.BlockSpec((tb, HEAD_DIM, LANES), lambda i: (i, 0, 0))
    ispec = pl.BlockSpec((tb, rl, LANES), lambda i: (i, 0, 0))
    return pl.pallas_call(
        functools.partial(_rwkv_scan_kernel, tb=tb),
        out_shape=[jax.ShapeDtypeStruct((s, rl, LANES), F32),
                   jax.ShapeDtypeStruct((HEAD_DIM, rl, LANES), F32)],
        grid=(s // tb,),
        in_specs=[jspec, jspec, ispec, jspec, jspec, jspec],
        out_specs=[ispec, pl.BlockSpec((HEAD_DIM, rl, LANES), lambda i: (0, 0, 0))],
        scratch_shapes=[pltpu.VMEM((HEAD_DIM, rl, LANES), F32)],
        compiler_params=_cparams(("arbitrary",)),
        name="rwkv_scan",
    )(rs, ks, vs, ws, kns, bs)


def _to_scan_j(x, ip):
    b, s, _ = x.shape
    t = x.reshape(b, s, N_HEADS, HEAD_DIM).transpose(1, 3, 0, 2).reshape(s, HEAD_DIM, b * N_HEADS)
    return jnp.concatenate([t] * ip, axis=-1)


def _to_scan_i(x, ip):
    b, s, _ = x.shape
    rl = HEAD_DIM // ip
    t = x.reshape(b, s, N_HEADS, ip, rl).transpose(1, 4, 3, 0, 2)
    return t.reshape(s, rl, ip * b * N_HEADS)


def _from_scan_i(y, b, ip):
    s, rl, _ = y.shape
    t = y.reshape(s, rl, ip, b, N_HEADS).transpose(3, 0, 4, 2, 1)
    return t.reshape(b, s, MIX_W)


def _state_from_scan(sf, b, ip):
    rl = sf.shape[1]
    t = sf.reshape(HEAD_DIM, rl, ip, b, N_HEADS).transpose(3, 4, 2, 1, 0)
    return t.reshape(b, N_HEADS, HEAD_DIM, HEAD_DIM)


def _rwkv_step_kernel(s_ref, r_ref, k_ref, v_ref, w_ref, kn_ref, b_ref, snew_ref, y_ref):
    for h in range(N_HEADS):
        s = s_ref[0, h]
        sa = -jnp.sum(s * kn_ref[0, h], axis=-1, keepdims=True)
        sn = s * w_ref[0, h] + sa * b_ref[0, h] + v_ref[0, h] * k_ref[0, h]
        snew_ref[0, h] = sn
        y_ref[0, h] = jnp.sum(sn * r_ref[0, h], axis=-1, keepdims=True)


def _rwkv_step(state, r, k, v, w, kn, bb):
    b = state.shape[0]
    jrow = lambda x: x.reshape(b, N_HEADS, 1, HEAD_DIM)
    jspec = pl.BlockSpec((1, N_HEADS, 1, HEAD_DIM), lambda i: (i, 0, 0, 0))
    cspec = pl.BlockSpec((1, N_HEADS, HEAD_DIM, 1), lambda i: (i, 0, 0, 0))
    sspec = pl.BlockSpec((1, N_HEADS, HEAD_DIM, HEAD_DIM), lambda i: (i, 0, 0, 0))
    snew, y = pl.pallas_call(
        _rwkv_step_kernel,
        out_shape=[jax.ShapeDtypeStruct(state.shape, F32),
                   jax.ShapeDtypeStruct((b, N_HEADS, HEAD_DIM, 1), F32)],
        grid=(b,),
        in_specs=[sspec, jspec, jspec, cspec, jspec, jspec, jspec],
        out_specs=[sspec, cspec],
        compiler_params=_cparams(("parallel",)),
        name="rwkv_step",
    )(state, jrow(r), jrow(k), v.reshape(b, N_HEADS, HEAD_DIM, 1), jrow(w), jrow(kn), jrow(bb))
    return snew, y.reshape(b, 1, MIX_W)


def _rwkv_post_kernel(y_ref, bonus_ref, g_ref, lnw_ref, lnb_ref, ones_ref, o_ref):
    y = y_ref[0]
    inv = 1.0 / HEAD_DIM
    mean = _dot3(y, ones_ref[...]) * inv
    yc = y - mean
    var = _dot3(yc * yc, ones_ref[...]) * inv
    yn = yc * lax.rsqrt(var + GN_EPS) * lnw_ref[...] + lnb_ref[...]
    o_ref[0] = (yn + bonus_ref[0]) * _silu(g_ref[0])


def _rwkv_post(y, bonus, z3, p, ts):
    b, s, _ = y.shape
    blk = pl.BlockSpec((1, ts, MIX_W), lambda i, j: (i, j, 0))
    row = lambda i, j: (0, 0)
    return pl.pallas_call(
        _rwkv_post_kernel,
        out_shape=jax.ShapeDtypeStruct((b, s, MIX_W), F32),
        grid=(b, s // ts),
        in_specs=[blk, blk,
                  pl.BlockSpec((1, ts, MIX_W), lambda i, j: (i, j, C_GRW // MIX_W)),
                  pl.BlockSpec((1, MIX_W), row), pl.BlockSpec((1, MIX_W), row),
                  pl.BlockSpec((MIX_W, MIX_W), row)],
        out_specs=blk,
        compiler_params=_cparams(("parallel", "parallel")),
        name="rwkv_post",
    )(y, bonus, z3, p["ln_w"], p["ln_b"], p["ones"])


def _logf(x, fb):
    u = x + fb
    return jnp.minimum(u, 0.0) - jnp.log(1.0 + jnp.exp(-jnp.abs(u)))


def _logf_kernel(f_ref, fb_ref, tri_ref, lf_ref, cum_ref, cumt_ref, carry_ref):
    @pl.when(pl.program_id(1) == 0)
    def _():
        carry_ref[...] = jnp.zeros_like(carry_ref)

    lf = _logf(f_ref[0], fb_ref[...])
    lf_ref[0] = lf
    hi, mid, lo = _split3(lf)
    tri = tri_ref[...]
    cum = _dot(tri, hi) + _dot(tri, mid) + _dot(tri, lo) + carry_ref[...]
    cum_ref[0] = cum
    cumt_ref[0] = cum.T
    carry_ref[...] = cum[cum.shape[0] - 1:, :]


def _fox_logf(z3, fbias_row, ts):
    b, s, _ = z3.shape
    r = lax.broadcasted_iota(jnp.int32, (ts, ts), 0)
    c = lax.broadcasted_iota(jnp.int32, (ts, ts), 1)
    tri = (c <= r).astype(BF16)
    blk = pl.BlockSpec((1, ts, LANES), lambda i, j: (i, j, 0))
    return pl.pallas_call(
        _logf_kernel,
        out_shape=[jax.ShapeDtypeStruct((b, s, LANES), F32),
                   jax.ShapeDtypeStruct((b, s, LANES), F32),
                   jax.ShapeDtypeStruct((b, LANES, s), F32)],
        grid=(b, s // ts),
        in_specs=[pl.BlockSpec((1, ts, LANES), lambda i, j: (i, j, C_FXF // LANES)),
                  pl.BlockSpec((1, LANES), lambda i, j: (0, 0)),
                  pl.BlockSpec((ts, ts), lambda i, j: (0, 0))],
        out_specs=[blk, blk, pl.BlockSpec((1, LANES, ts), lambda i, j: (i, 0, j))],
        scratch_shapes=[pltpu.VMEM((1, LANES), F32)],
        compiler_params=_cparams(("parallel", "arbitrary")),
        name="fox_logf",
    )(z3, fbias_row, tri)


def _head_masked_q(q_ref, h):
    g = h // 2
    q = q_ref[0, :, g * LANES:(g + 1) * LANES] * ATTN_SCALE
    if q.shape[0] < SUBLANES:
        q = jnp.broadcast_to(q[0:1], (SUBLANES, LANES))
    lane = lax.broadcasted_iota(jnp.int32, q.shape, 1)
    keep = (lane >= HEAD_DIM) if (h % 2) else (lane < HEAD_DIM)
    return jnp.where(keep, q, 0.0).astype(BF16)


def _pair_select(lo_half, hi_half):
    lane = lax.broadcasted_iota(jnp.int32, lo_half.shape, 1)
    return jnp.where(lane < HEAD_DIM, lo_half, hi_half)


def _pair_q(q_ref, g):
    q = q_ref[0, :, g * LANES:(g + 1) * LANES] * ATTN_SCALE
    lane = lax.broadcasted_iota(jnp.int32, q.shape, 1)
    return jnp.concatenate([jnp.where(lane < HEAD_DIM, q, 0.0),
                            jnp.where(lane >= HEAD_DIM, q, 0.0)], axis=0).astype(BF16)


def _lane_tile(x, width):
    return jnp.concatenate([x] * (width // LANES), axis=1)


def _sb_prompt_kernel(q_ref, k_ref, v_ref, g_ref, tri_ref, o_ref, acc_ref, carry_ref, *, t):
    qi = pl.program_id(1)
    kj = pl.program_id(2)

    @pl.when(kj == 0)
    def _():
        acc_ref[...] = jnp.zeros_like(acc_ref)
        carry_ref[...] = jnp.zeros_like(carry_ref)

    def block(diagonal):
        kb = k_ref[0].astype(BF16)
        vb = v_ref[0].astype(BF16)
        if diagonal:
            mask = (lax.broadcasted_iota(jnp.int32, (t, t), 1)
                    < lax.broadcasted_iota(jnp.int32, (t, t), 0))
        zs = []
        for g in range(N_HEADS // 2):
            zz = _dot_nt(_pair_q(q_ref, g), kb[:, g * LANES:(g + 1) * LANES])
            zs += [zz[:t], zz[t:]]
        sps = [_softplus(z) for z in zs]
        if diagonal:
            sps = [jnp.where(mask, sp, 0.0) for sp in sps]
        sp_all = jnp.concatenate(sps, axis=0)
        hi = sp_all.astype(BF16)
        lo = (sp_all - hi.astype(F32)).astype(BF16)
        incl = _dot(jnp.concatenate([hi, lo], axis=1), tri_ref[...])
        for g in range(N_HEADS // 2):
            aa = []
            for h in (2 * g, 2 * g + 1):
                inc_h = incl[h * t:(h + 1) * t]
                c = carry_ref[h]
                a = jnp.exp(zs[h] - inc_h - _lane_tile(c, t))
                if diagonal:
                    a = jnp.where(mask, a, 0.0)
                aa.append(a.astype(BF16))
                carry_ref[h] = c + jnp.broadcast_to(inc_h[:, 0:1], (t, LANES))
            oo = _dot(jnp.concatenate(aa, axis=0), vb[:, g * LANES:(g + 1) * LANES])
            acc_ref[:, g * LANES:(g + 1) * LANES] += _pair_select(oo[:t], oo[t:])

    @pl.when(kj == 0)
    def _():
        block(True)

    @pl.when(jnp.logical_and(kj > 0, kj <= qi))
    def _():
        block(False)

    @pl.when(kj == pl.num_programs(2) - 1)
    def _():
        o_ref[0] = acc_ref[...] * _silu(g_ref[0])


def _sb_prompt(z3, t):
    b, s, _ = z3.shape
    r = lax.broadcasted_iota(jnp.int32, (2 * t, t), 0) % t
    c = lax.broadcasted_iota(jnp.int32, (2 * t, t), 1)
    tri = (r >= c).astype(BF16)
    qblk = lambda col: pl.BlockSpec((1, t, MIX_W), lambda i, j, k: (i, j, col // MIX_W))
    kblk = lambda col: pl.BlockSpec(
        (1, t, MIX_W), lambda i, j, k: (i, jnp.maximum(j - k, 0), col // MIX_W))
    return pl.pallas_call(
        functools.partial(_sb_prompt_kernel, t=t),
        out_shape=jax.ShapeDtypeStruct((b, s, MIX_W), F32),
        grid=(b, s // t, s // t),
        in_specs=[qblk(C_SBQ), kblk(C_SBK), kblk(C_SBV), qblk(C_GSB),
                  pl.BlockSpec((2 * t, t), lambda i, j, k: (0, 0))],
        out_specs=pl.BlockSpec((1, t, MIX_W), lambda i, j, k: (i, j, 0)),
        scratch_shapes=[pltpu.VMEM((t, MIX_W), F32), pltpu.VMEM((N_HEADS, t, LANES), F32)],
        compiler_params=_cparams(("parallel", "parallel", "arbitrary")),
        name="sb_prompt",
    )(z3, z3, z3, z3, tri)


def _fox_prompt_kernel(q_ref, k_ref, v_ref, g_ref, fq_ref, fk_ref, o_ref,
                       acc_ref, m_ref, l_ref, fqb_ref, *, t):
    qi = pl.program_id(1)
    kj = pl.program_id(2)

    @pl.when(kj == 0)
    def _():
        acc_ref[...] = jnp.zeros_like(acc_ref)
        m_ref[...] = jnp.full_like(m_ref, -jnp.inf)
        l_ref[...] = jnp.zeros_like(l_ref)
        fq = fq_ref[0]
        for h in range(N_HEADS):
            fqb_ref[h] = jnp.broadcast_to(fq[:, h:h + 1], (t, LANES))

    def block(diagonal):
        kb = k_ref[0].astype(BF16)
        vb = v_ref[0].astype(BF16)
        fk = fk_ref[0]
        if diagonal:
            mask = (lax.broadcasted_iota(jnp.int32, (t, t), 1)
                    <= lax.broadcasted_iota(jnp.int32, (t, t), 0))
        for g in range(N_HEADS // 2):
            zz = _dot_nt(_pair_q(q_ref, g), kb[:, g * LANES:(g + 1) * LANES])
            ps, alphas = [], []
            for idx, h in enumerate((2 * g, 2 * g + 1)):
                z = zz[idx * t:(idx + 1) * t] + _lane_tile(fqb_ref[h], t) - fk[h:h + 1, :]
                if diagonal:
                    z = jnp.where(mask, z, -jnp.inf)
                m_old = m_ref[h]
                m_new = jnp.maximum(
                    m_old, jnp.broadcast_to(jnp.max(z, axis=-1, keepdims=True), (t, LANES)))
                alpha = jnp.exp(m_old - m_new)
                p = jnp.exp(z - _lane_tile(m_new, t))
                l_ref[h] = l_ref[h] * alpha + jnp.broadcast_to(
                    jnp.sum(p, axis=-1, keepdims=True), (t, LANES))
                m_ref[h] = m_new
                ps.append(p.astype(BF16))
                alphas.append(alpha)
            oo = _dot(jnp.concatenate(ps, axis=0), vb[:, g * LANES:(g + 1) * LANES])
            sl = slice(g * LANES, (g + 1) * LANES)
            acc_ref[:, sl] = (acc_ref[:, sl] * _pair_select(alphas[0], alphas[1])
                              + _pair_select(oo[:t], oo[t:]))

    @pl.when(kj < qi)
    def _():
        block(False)

    @pl.when(kj == qi)
    def _():
        block(True)

    @pl.when(kj == pl.num_programs(2) - 1)
    def _():
        for g in range(N_HEADS // 2):
            sl = slice(g * LANES, (g + 1) * LANES)
            inv = _pair_select(1.0 / l_ref[2 * g], 1.0 / l_ref[2 * g + 1])
            o_ref[0, :, sl] = acc_ref[:, sl] * inv * _silu(g_ref[0, :, sl])


def _fox_prompt(z3, cum, cumt, t):
    b, s, _ = z3.shape
    qblk = lambda col: pl.BlockSpec((1, t, MIX_W), lambda i, j, k: (i, j, col // MIX_W))
    kblk = lambda col: pl.BlockSpec(
        (1, t, MIX_W), lambda i, j, k: (i, jnp.minimum(k, j), col // MIX_W))
    stat = pltpu.VMEM((N_HEADS, t, LANES), F32)
    return pl.pallas_call(
        functools.partial(_fox_prompt_kernel, t=t),
        out_shape=jax.ShapeDtypeStruct((b, s, MIX_W), F32),
        grid=(b, s // t, s // t),
        in_specs=[qblk(C_FXQ), kblk(C_FXK), kblk(C_FXV), qblk(C_GFX),
                  pl.BlockSpec((1, t, LANES), lambda i, j, k: (i, j, 0)),
                  pl.BlockSpec((1, SUBLANES, t), lambda i, j, k: (i, 0, jnp.minimum(k, j)))],
        out_specs=pl.BlockSpec((1, t, MIX_W), lambda i, j, k: (i, j, 0)),
        scratch_shapes=[pltpu.VMEM((t, MIX_W), F32), stat, stat, stat],
        compiler_params=_cparams(("parallel", "parallel", "arbitrary")),
        name="fox_prompt",
    )(z3, z3, z3, z3, cum, cumt)


def _mem_kernel(q_ref, g_ref, mk_ref, mv_ref, o_ref):
    kb = mk_ref[0].astype(BF16)
    vb = mv_ref[0].astype(BF16)
    for g in range(MEM_HEADS // 2):
        outs = []
        for h in (2 * g, 2 * g + 1):
            z = _dot_nt(_head_masked_q(q_ref, h), kb[:, g * LANES:(g + 1) * LANES])
            m = jnp.max(z, axis=-1, keepdims=True)
            e = jnp.exp(z - m)
            p = e / jnp.sum(e, axis=-1, keepdims=True)
            outs.append(_dot(p.astype(BF16), vb[:, g * LANES:(g + 1) * LANES]))
        sl = slice(g * LANES, (g + 1) * LANES)
        o = _pair_select(outs[0], outs[1])
        o_ref[0, :, sl] = o[:o_ref.shape[1]] * _silu(g_ref[0, :, sl])


def _mem_attend(z3, mk, mv, tq):
    b, s, _ = z3.shape
    n_mem = mk.shape[1]
    kv = pl.BlockSpec((1, n_mem, MEM_W), lambda i, j: (i, 0, 0))
    return pl.pallas_call(
        _mem_kernel,
        out_shape=jax.ShapeDtypeStruct((b, s, MEM_W), F32),
        grid=(b, s // tq),
        in_specs=[pl.BlockSpec((1, tq, MEM_W), lambda i, j: (i, j, C_MQ // MEM_W)),
                  pl.BlockSpec((1, tq, MEM_W), lambda i, j: (i, j, C_GM // MEM_W)),
                  kv, kv],
        out_specs=pl.BlockSpec((1, tq, MEM_W), lambda i, j: (i, j, 0)),
        compiler_params=_cparams(("parallel", "parallel")),
        name="mem_attend",
    )(z3, z3, mk, mv)


def _merge_kernel(x_ref, yrw_ref, ysb_ref, yfx_ref, ym_ref, g0_ref, g1_ref, g2_ref, g3_ref,
                  wrw_ref, wsb_ref, wfx_ref, wm_ref, wout_ref, fg_ref, xo_ref, yo_ref):
    def branch(y_ref, w_ref, gate_ref):
        return _sigmoid(gate_ref[0]) * _dot(y_ref[0].astype(BF16), w_ref[...])

    merged = (branch(yrw_ref, wrw_ref, g0_ref) + branch(ysb_ref, wsb_ref, g1_ref)
              + branch(yfx_ref, wfx_ref, g2_ref) + branch(ym_ref, wm_ref, g3_ref))
    xn = x_ref[0] + _dot(merged.astype(BF16), wout_ref[...])
    xo_ref[0] = xn
    ms = jnp.mean(xn * xn, axis=-1, keepdims=True)
    yo_ref[0] = xn * lax.rsqrt(ms + NORM_EPS) * fg_ref[...]


def _merge(x3, yrw, ysb, yfx, ym, z3, wts, final_g, tm):
    b, s, d = x3.shape
    row = lambda i, j: (0, 0)
    blk = lambda w: pl.BlockSpec((1, tm, w), lambda i, j: (i, j, 0))
    gate = lambda n: pl.BlockSpec((1, tm, d), lambda i, j: (i, j, C_GATE // d + n))
    full = lambda a: pl.BlockSpec(a.shape, row)
    return pl.pallas_call(
        _merge_kernel,
        out_shape=[jax.ShapeDtypeStruct(x3.shape, F32)] * 2,
        grid=(b, s // tm),
        in_specs=[blk(d), blk(MIX_W), blk(MIX_W), blk(MIX_W), blk(MEM_W),
                  gate(0), gate(1), gate(2), gate(3),
                  full(wts["rw"]), full(wts["sb"]), full(wts["fx"]), full(wts["m"]),
                  full(wts["out"]), pl.BlockSpec((1, d), row)],
        out_specs=[blk(d), blk(d)],
        compiler_params=_cparams(("parallel", "parallel")),
        name="merge",
    )(x3, yrw, ysb, yfx, ym, z3, z3, z3, z3,
      wts["rw"], wts["sb"], wts["fx"], wts["m"], wts["out"], final_g.reshape(1, d))


def _block_diag_q(q_row):
    q = jnp.broadcast_to(q_row * ATTN_SCALE, (N_HEADS, MIX_W))
    return jnp.where(_head_rows(), q, 0.0).astype(BF16)


def _head_rows():
    r = lax.broadcasted_iota(jnp.int32, (N_HEADS, MIX_W), 0)
    c = lax.broadcasted_iota(jnp.int32, (N_HEADS, MIX_W), 1) // HEAD_DIM
    return r == c


def _collapse_heads(acc):
    return jnp.sum(jnp.where(_head_rows(), acc, 0.0), axis=0, keepdims=True)


def _page_group(n_pages):
    return next(g for g in (8, 4, 2, 1) if n_pages % g == 0)


def _pages_keys_minor(cache):
    l, n, page = cache.shape[:3]
    return jnp.transpose(cache, (0, 1, 3, 4, 2)).reshape(l, n, -1, page)


def _pool_specs(grp, n_pages, layer, rows, page):
    def spec(g):
        return pl.BlockSpec((None, None, rows, page),
                            lambda i, p, pt: (layer, pt[i, n_pages - 1 - (p * grp + g)], 0, 0))
    return [spec(g) for g in range(grp)]


def _suffix_total_matrix(pieces, page, inclusive):
    r = lax.broadcasted_iota(jnp.int32, (pieces * page, 2 * page), 0) % page
    c = lax.broadcasted_iota(jnp.int32, (pieces * page, 2 * page), 1)
    return jnp.where(c < page, (r >= c) if inclusive else (r > c), True).astype(BF16)


def _sb_decode_kernel(pt_ref, q_ref, kn_ref, vn_ref, g_ref, *rest, page, grp):
    k_refs, v_refs = rest[:grp], rest[grp:2 * grp]
    tri_ref, o_ref, qbd_ref, acc_ref, carry_ref = rest[2 * grp:]
    p = pl.program_id(1)
    n_steps = pl.num_programs(1)

    @pl.when(p == 0)
    def _():
        qbd = _block_diag_q(q_ref[0])
        qbd_ref[...] = qbd
        new_pos = n_steps * (grp * page)
        new_mask = new_pos < new_pos
        z_new = jnp.sum(qbd.astype(F32) * kn_ref[0].astype(BF16).astype(F32), axis=-1, keepdims=True)
        sp_new = jnp.where(new_mask, _softplus(z_new), 0.0)
        a_new = jnp.where(new_mask, jnp.exp(z_new - sp_new), 0.0)
        acc_ref[...] = a_new.astype(BF16).astype(F32) * vn_ref[0].astype(BF16).astype(F32)
        carry_ref[...] = jnp.broadcast_to(sp_new, carry_ref.shape)

    qbd = qbd_ref[...]
    zs = [_dot(qbd, k_refs[g][...].astype(BF16)) for g in range(grp)]
    sp_all = jnp.concatenate([_softplus(z) for z in zs], axis=0)
    hi = sp_all.astype(BF16)
    lo = (sp_all - hi.astype(F32)).astype(BF16)
    both = _dot(jnp.concatenate([hi, lo], axis=1), tri_ref[...])
    carry = carry_ref[...]
    acc = acc_ref[...]
    for g in range(grp):
        rows = slice(g * N_HEADS, (g + 1) * N_HEADS)
        a = jnp.exp(zs[g] - both[rows, :page] - carry)
        acc = acc + _dot_nt(a.astype(BF16), v_refs[g][...].astype(BF16))
        carry = carry + both[rows, page:]
    acc_ref[...] = acc
    carry_ref[...] = carry

    @pl.when(p == n_steps - 1)
    def _():
        o_ref[0] = _collapse_heads(acc_ref[...]) * _silu(g_ref[0])


def _sb_decode(zs3, cache_k, cache_v, page_table, layer):
    b = zs3.shape[0]
    n_pages = page_table.shape[1]
    page = cache_k.shape[2]
    grp = _page_group(n_pages)
    row = lambda col: pl.BlockSpec((1, 1, MIX_W), lambda i, p, pt: (i, 0, col // MIX_W))
    pools = _pool_specs(grp, n_pages, layer, MIX_W, page)
    tri = _suffix_total_matrix(2, page, True)
    grid_spec = pltpu.PrefetchScalarGridSpec(
        num_scalar_prefetch=1,
        grid=(b, n_pages // grp),
        in_specs=[row(C_SBQ), row(C_SBK), row(C_SBV), row(C_GSB)] + pools + pools
                 + [pl.BlockSpec(tri.shape, lambda i, p, pt: (0, 0))],
        out_specs=pl.BlockSpec((1, 1, MIX_W), lambda i, p, pt: (i, 0, 0)),
        scratch_shapes=[pltpu.VMEM((N_HEADS, MIX_W), BF16), pltpu.VMEM((N_HEADS, MIX_W), F32),
                        pltpu.VMEM((N_HEADS, page), F32)],
    )
    kt = _pages_keys_minor(cache_k)
    vt = _pages_keys_minor(cache_v)
    return pl.pallas_call(
        functools.partial(_sb_decode_kernel, page=page, grp=grp),
        out_shape=jax.ShapeDtypeStruct((b, 1, MIX_W), F32),
        grid_spec=grid_spec,
        compiler_params=_cparams(("parallel", "arbitrary")),
        name="sb_decode",
    )(page_table, zs3, zs3, zs3, zs3, *([kt] * grp), *([vt] * grp), tri)


def _fox_decode_kernel(pt_ref, q_ref, kn_ref, vn_ref, g_ref, f_ref, fb_ref, *rest, page, grp):
    k_refs, v_refs, lf_refs = rest[:grp], rest[grp:2 * grp], rest[2 * grp:3 * grp]
    tri_ref, o_ref, lf_ref, qbd_ref, acc_ref, carry_ref, m_ref, l_ref = rest[3 * grp:]
    p = pl.program_id(1)
    n_steps = pl.num_programs(1)

    @pl.when(p == 0)
    def _():
        qbd = _block_diag_q(q_ref[0])
        qbd_ref[...] = qbd
        lf_new = _logf(f_ref[0], fb_ref[...])
        lf_ref[0] = lf_new
        r = lax.broadcasted_iota(jnp.int32, (N_HEADS, LANES), 0)
        c = lax.broadcasted_iota(jnp.int32, (N_HEADS, LANES), 1)
        lf_col = jnp.sum(jnp.where(r == c, jnp.broadcast_to(lf_new, (N_HEADS, LANES)), 0.0),
                         axis=-1, keepdims=True)
        carry_ref[...] = jnp.broadcast_to(lf_col, carry_ref.shape)
        z_new = jnp.sum(qbd.astype(F32) * kn_ref[0].astype(BF16).astype(F32), axis=-1, keepdims=True)
        m_ref[...] = z_new
        l_ref[...] = jnp.ones_like(l_ref)
        acc_ref[...] = jnp.broadcast_to(vn_ref[0].astype(BF16).astype(F32), acc_ref.shape)

    lf_all = jnp.concatenate([lf_refs[g][...] for g in range(grp)], axis=0)
    both = _dot(jnp.concatenate(_split3(lf_all), axis=1), tri_ref[...])
    qbd = qbd_ref[...]
    carry = carry_ref[...]
    zs = []
    for g in range(grp):
        rows = slice(g * N_HEADS, (g + 1) * N_HEADS)
        zs.append(_dot(qbd, k_refs[g][...].astype(BF16)) + both[rows, :page] + carry)
        carry = carry + both[rows, page:]
    carry_ref[...] = carry
    z = jnp.concatenate(zs, axis=1)
    m_old = m_ref[...]
    m_new = jnp.maximum(m_old, jnp.max(z, axis=-1, keepdims=True))
    alpha = jnp.exp(m_old - m_new)
    pr = jnp.exp(z - m_new)
    l_ref[...] = l_ref[...] * alpha + jnp.sum(pr, axis=-1, keepdims=True)
    m_ref[...] = m_new
    acc = acc_ref[...] * alpha
    for g in range(grp):
        acc = acc + _dot_nt(pr[:, g * page:(g + 1) * page].astype(BF16),
                            v_refs[g][...].astype(BF16))
    acc_ref[...] = acc

    @pl.when(p == n_steps - 1)
    def _():
        o_ref[0] = _collapse_heads(acc_ref[...] / l_ref[...]) * _silu(g_ref[0])


def _fox_decode(zs3, cache_k, cache_v, cache_logf, fbias_row, page_table, layer):
    b = zs3.shape[0]
    n_pages = page_table.shape[1]
    page = cache_k.shape[2]
    grp = _page_group(n_pages)
    row = lambda col: pl.BlockSpec((1, 1, MIX_W), lambda i, p, pt: (i, 0, col // MIX_W))
    pools = _pool_specs(grp, n_pages, layer, MIX_W, page)
    tri = _suffix_total_matrix(3, page, False)
    grid_spec = pltpu.PrefetchScalarGridSpec(
        num_scalar_prefetch=1,
        grid=(b, n_pages // grp),
        in_specs=[row(C_FXQ), row(C_FXK), row(C_FXV), row(C_GFX),
                  pl.BlockSpec((1, 1, LANES), lambda i, p, pt: (i, 0, C_FXF // LANES)),
                  pl.BlockSpec((1, LANES), lambda i, p, pt: (0, 0))]
                 + pools + pools + _pool_specs(grp, n_pages, layer, N_HEADS, page)
                 + [pl.BlockSpec(tri.shape, lambda i, p, pt: (0, 0))],
        out_specs=[pl.BlockSpec((1, 1, MIX_W), lambda i, p, pt: (i, 0, 0)),
                   pl.BlockSpec((1, 1, LANES), lambda i, p, pt: (i, 0, 0))],
        scratch_shapes=[pltpu.VMEM((N_HEADS, MIX_W), BF16), pltpu.VMEM((N_HEADS, MIX_W), F32),
                        pltpu.VMEM((N_HEADS, page), F32),
                        pltpu.VMEM((N_HEADS, 1), F32), pltpu.VMEM((N_HEADS, 1), F32)],
    )
    kt = _pages_keys_minor(cache_k)
    vt = _pages_keys_minor(cache_v)
    lft = jnp.transpose(cache_logf, (0, 1, 3, 2))
    return pl.pallas_call(
        functools.partial(_fox_decode_kernel, page=page, grp=grp),
        out_shape=[jax.ShapeDtypeStruct((b, 1, MIX_W), F32),
                   jax.ShapeDtypeStruct((b, 1, LANES), F32)],
        grid_spec=grid_spec,
        compiler_params=_cparams(("parallel", "arbitrary")),
        name="fox_decode",
    )(page_table, zs3, zs3, zs3, zs3, zs3, fbias_row,
      *([kt] * grp), *([vt] * grp), *([lft] * grp), tri)


def _reorder_w_in(w):
    c_grw = RWKV_SHIFT_W
    c_fxf = c_grw + 9 * MIX_W
    c_mq = c_fxf + N_HEADS
    pad = jnp.zeros((w.shape[0], RW_BLOCK_W - RWKV_SHIFT_W - N_HEADS), w.dtype)
    return jnp.concatenate(
        [w[:, :RWKV_SHIFT_W], w[:, c_fxf:c_mq], pad, w[:, c_grw:c_fxf], w[:, c_mq:]],
        axis=1).astype(BF16)


def _pick(n, pref):
    return pref if n % pref == 0 else n


def _heads(x, b, s):
    return x.reshape(b, s, -1, HEAD_DIM)


def kernel(x_prompt, x_sample, cache_sb_k, cache_sb_v, cache_fox_k, cache_fox_v, cache_fox_logf, cache_mem_k, cache_mem_v, state_rwkv_wkv, state_rwkv_shift, page_table, mem_prompt, norm_g, w_in, rwkv_mu, rwkv_w0, rwkv_w_up, rwkv_a0, rwkv_a_up, rwkv_k_k, rwkv_k_a, rwkv_r_k, rwkv_ln_w, rwkv_ln_b, fox_fbias, mem_norm_g, w_mem_kv, w_rwkv_o, w_sb_o, w_fox_o, w_mem_o, w_out, final_norm_g):
    depth = w_in.shape[0]
    bp, sp, d = x_prompt.shape
    bs, ss, _ = x_sample.shape
    assert ss == 1, "the sample group carries one new token per sequence"
    n_mem = mem_prompt.shape[1]
    bh = bp * N_HEADS
    ip = LANES // bh
    ones = _head_ones(MIX_W)
    zero_shift = jnp.zeros((bp, 1, RWKV_SHIFT_W), F32)

    hp, hs = x_prompt, x_sample
    yp = ys = None
    p_new = [[] for _ in range(9)]
    s_new = [[] for _ in range(7)]
    for l in range(depth):
        w_l = _reorder_w_in(w_in[l])
        rp = {
            "mu": rwkv_mu[l].reshape(1, -1), "w0": rwkv_w0[l].reshape(1, -1),
            "a0": rwkv_a0[l].reshape(1, -1), "k_k": rwkv_k_k[l].reshape(1, -1),
            "k_a": rwkv_k_a[l].reshape(1, -1), "r_k": rwkv_r_k[l].reshape(1, -1),
            "ln_w": rwkv_ln_w[l].reshape(1, -1), "ln_b": rwkv_ln_b[l].reshape(1, -1),
            "w_up": jnp.concatenate([rwkv_w_up[l], jnp.zeros_like(rwkv_a_up[l])], 0).astype(BF16),
            "a_up": jnp.concatenate([jnp.zeros_like(rwkv_w_up[l]), rwkv_a_up[l]], 0).astype(BF16),
            "ones": ones,
        }
        wts = {"rw": w_rwkv_o[l].astype(BF16), "sb": w_sb_o[l].astype(BF16),
               "fx": w_fox_o[l].astype(BF16), "m": w_mem_o[l].astype(BF16),
               "out": w_out[l].astype(BF16)}
        fbias_row = jnp.zeros((1, LANES), F32).at[0, :N_HEADS].set(fox_fbias[l])

        kvm = _norm_proj(mem_prompt.reshape(bp * n_mem, d), mem_norm_g[l],
                         w_mem_kv[l].astype(BF16), 2 * MEM_W).reshape(bp, n_mem, 2 * MEM_W)
        mk, mv = kvm[..., :MEM_W], kvm[..., MEM_W:]
        z = _norm_proj(hp.reshape(bp * sp, d), norm_g[l], w_l, 1024).reshape(bp, sp, N_PROJ)
        ts = _pick(sp, 256)
        r_, k_, v_, w_, kn_, b_, bonus = _rwkv_prep(z, zero_shift, rp, ts, True)
        y_s, s_fin = _rwkv_scan(_to_scan_j(r_, ip), _to_scan_j(k_, ip), _to_scan_i(v_, ip),
                                _to_scan_j(w_, ip), _to_scan_j(kn_, ip), _to_scan_j(b_, ip),
                                _pick(sp, 32))
        y_rw = _rwkv_post(_from_scan_i(y_s, bp, ip), bonus, z, rp, ts)
        y_sb = _sb_prompt(z, ts)
        logf, cum, cumt = _fox_logf(z, fbias_row, ts)
        y_fx = _fox_prompt(z, cum, cumt, ts)
        y_m = _mem_attend(z, mk, mv, ts)
        hp, yp = _merge(hp, y_rw, y_sb, y_fx, y_m, z, wts, final_norm_g, ts)
        st = (_heads(z[..., C_SBK:C_SBK + MIX_W], bp, sp), _heads(z[..., C_SBV:C_SBV + MIX_W], bp, sp),
              _heads(z[..., C_FXK:C_FXK + MIX_W], bp, sp), _heads(z[..., C_FXV:C_FXV + MIX_W], bp, sp),
              logf[..., :N_HEADS], _state_from_scan(s_fin, bp, ip), z[:, -1, :RWKV_SHIFT_W],
              _heads(mk, bp, n_mem), _heads(mv, bp, n_mem))
        for lst, arr in zip(p_new, st):
            lst.append(arr)

        zs = _norm_proj(hs.reshape(bs, d), norm_g[l], w_l, 1024)
        zs_rows = zs.reshape(1, bs, N_PROJ)
        zs_seq = zs.reshape(bs, 1, N_PROJ)
        r_, k_, v_, w_, kn_, b_, bonus = _rwkv_prep(
            zs_rows, state_rwkv_shift[l].reshape(1, bs, RWKV_SHIFT_W), rp, bs, False)
        s_next, y_s = _rwkv_step(state_rwkv_wkv[l], r_, k_, v_, w_, kn_, b_)
        y_rw = _rwkv_post(y_s.reshape(1, bs, MIX_W), bonus, zs_rows, rp, bs)
        y_sb = _sb_decode(zs_seq, cache_sb_k, cache_sb_v, page_table, l)
        y_fx, logf_s = _fox_decode(zs_seq, cache_fox_k, cache_fox_v, cache_fox_logf, fbias_row,
                                   page_table, l)
        y_m = _mem_attend(zs_seq, cache_mem_k[l].reshape(bs, n_mem, MEM_W),
                          cache_mem_v[l].reshape(bs, n_mem, MEM_W), 1)
        hs_rows, ys_rows = _merge(hs.reshape(1, bs, d), y_rw, y_sb.reshape(1, bs, MIX_W),
                                  y_fx.reshape(1, bs, MIX_W), y_m.reshape(1, bs, MEM_W),
                                  zs_rows, wts, final_norm_g, bs)
        hs, ys = hs_rows.reshape(bs, 1, d), ys_rows.reshape(bs, 1, d)
        st = (_heads(zs_seq[..., C_SBK:C_SBK + MIX_W], bs, 1), _heads(zs_seq[..., C_SBV:C_SBV + MIX_W], bs, 1),
              _heads(zs_seq[..., C_FXK:C_FXK + MIX_W], bs, 1), _heads(zs_seq[..., C_FXV:C_FXV + MIX_W], bs, 1),
              logf_s[..., :N_HEADS], s_next, zs[:, :RWKV_SHIFT_W])
        for lst, arr in zip(s_new, st):
            lst.append(arr)

    return (yp, ys) + tuple(jnp.stack(t) for t in p_new) + tuple(jnp.stack(t) for t in s_new)
```

```python
import functools

import jax
import jax.numpy as jnp
import numpy as np
from jax import lax
from jax.experimental import pallas as pl
from jax.experimental.pallas import tpu as pltpu

F32 = jnp.float32
BF16 = jnp.bfloat16

D_MODEL = 1024
HEAD_DIM = 64
N_HEADS = 8
MIX_W = N_HEADS * HEAD_DIM
MEM_HEADS = 4
MEM_W = MEM_HEADS * HEAD_DIM
DECAY_RANK = 64
ICLR_RANK = 64
N_BRANCH = 4
NORM_EPS = 1e-6
GN_EPS = 64e-5
ATTN_SCALE = HEAD_DIM ** -0.5
RWKV_SHIFT_W = 3 * MIX_W + DECAY_RANK + ICLR_RANK

LANES = 128
SUBLANES = 8
VMEM_LIMIT = 48 * 1024 * 1024

RW_BLOCK_W = 2048
C_RW = 0
C_FXF = RWKV_SHIFT_W
C_GRW = 2048
C_SBQ, C_SBK, C_SBV, C_GSB = 2560, 3072, 3584, 4096
C_FXQ, C_FXK, C_FXV, C_GFX = 4608, 5120, 5632, 6144
C_MQ, C_GM = 6656, 6912
C_GATE = 7168
N_PROJ = C_GATE + N_BRANCH * D_MODEL


def _cparams(sem):
    return pltpu.CompilerParams(dimension_semantics=sem, vmem_limit_bytes=VMEM_LIMIT)


def _split3(x):
    hi = x.astype(BF16)
    r1 = x - hi.astype(F32)
    mid = r1.astype(BF16)
    lo = (r1 - mid.astype(F32)).astype(BF16)
    return hi, mid, lo


def _dot(a, b):
    return jnp.dot(a, b, preferred_element_type=F32)


def _dot_nt(a, b):
    return lax.dot_general(a, b, (((1,), (1,)), ((), ())), preferred_element_type=F32)


def _dot3(x, ones):
    hi, mid, lo = _split3(x)
    return _dot(hi, ones) + _dot(mid, ones) + _dot(lo, ones)


def _softplus(x):
    return jnp.maximum(x, 0.0) + jnp.log(1.0 + jnp.exp(-jnp.abs(x)))


def _sigmoid(x):
    return 1.0 / (1.0 + jnp.exp(-x))


def _silu(x):
    return x * _sigmoid(x)


def _head_ones(width):
    r = lax.broadcasted_iota(jnp.int32, (width, width), 0) // HEAD_DIM
    c = lax.broadcasted_iota(jnp.int32, (width, width), 1) // HEAD_DIM
    return (r == c).astype(BF16)


def _proj_kernel(x_ref, g_ref, w_ref, o_ref, xn_ref):
    @pl.when(pl.program_id(1) == 0)
    def _():
        x = x_ref[...]
        ms = jnp.mean(x * x, axis=-1, keepdims=True)
        xn_ref[...] = (x * lax.rsqrt(ms + NORM_EPS) * g_ref[...]).astype(BF16)

    o_ref[...] = _dot(xn_ref[...], w_ref[...])


def _norm_proj(x2d, g, w_bf16, tn):
    m, d = x2d.shape
    n = w_bf16.shape[1]
    tm = next((c for c in (1024, 512, 256) if m % c == 0), m)
    assert n % tn == 0
    return pl.pallas_call(
        _proj_kernel,
        out_shape=jax.ShapeDtypeStruct((m, n), F32),
        grid=(m // tm, n // tn),
        in_specs=[
            pl.BlockSpec((tm, d), lambda i, j: (i, 0)),
            pl.BlockSpec((1, d), lambda i, j: (0, 0)),
            pl.BlockSpec((d, tn), lambda i, j: (0, j)),
        ],
        out_specs=pl.BlockSpec((tm, tn), lambda i, j: (i, j)),
        scratch_shapes=[pltpu.VMEM((tm, d), BF16)],
        compiler_params=_cparams(("parallel", "arbitrary")),
        name="norm_proj",
    )(x2d, g.reshape(1, d), w_bf16)


def _rwkv_prep_kernel(z_ref, zprev_ref, sh0_ref, mu_ref, w0_ref, a0_ref, kk_ref, ka_ref, rk_ref,
                      wup_ref, aup_ref, ones_ref,
                      r_out, k_out, v_out, w_out, kn_out, b_out, bonus_out, *, ts, seq):
    z = z_ref[0, :, 0:RWKV_SHIFT_W]
    if seq:
        first = jnp.where(pl.program_id(1) == 0, sh0_ref[0],
                          zprev_ref[0, SUBLANES - 1:SUBLANES, 0:RWKV_SHIFT_W])
        row = lax.broadcasted_iota(jnp.int32, (ts, 1), 0)
        prev = jnp.where(row == 0, first, pltpu.roll(z, 1, axis=0))
    else:
        prev = sh0_ref[0]
    zm = z + (prev - z) * mu_ref[...]
    r = zm[:, 0:MIX_W]
    k = zm[:, MIX_W:2 * MIX_W]
    v = zm[:, 2 * MIX_W:3 * MIX_W]
    lowrank = zm[:, 3 * MIX_W:RWKV_SHIFT_W]
    u_w = _dot(jnp.tanh(lowrank).astype(BF16), wup_ref[...])
    u_a = _dot(lowrank.astype(BF16), aup_ref[...])
    w_log = -_softplus(-(w0_ref[...] + u_w)) - 0.5
    decay = jnp.exp(-jnp.exp(w_log))
    a = _sigmoid(a0_ref[...] + u_a)
    kn = k * kk_ref[...]
    ss = _dot3(kn * kn, ones_ref[...])
    kn = kn / jnp.maximum(jnp.sqrt(ss), 1e-12)
    k2 = k * (1.0 + (a - 1.0) * ka_ref[...])
    bonus = _dot3(r * k2 * rk_ref[...], ones_ref[...]) * v
    r_out[0] = r
    k_out[0] = k2
    v_out[0] = v
    w_out[0] = decay
    kn_out[0] = kn
    b_out[0] = kn * a
    bonus_out[0] = bonus


def _rwkv_prep(z3, shift, p, ts, seq):
    b, s, _ = z3.shape
    row = lambda i, j: (0, 0)
    vec = lambda n: pl.BlockSpec((1, n), row)
    out_sds = jax.ShapeDtypeStruct((b, s, MIX_W), F32)
    out_spec = pl.BlockSpec((1, ts, MIX_W), lambda i, j: (i, j, 0))
    if seq:
        shift_spec = pl.BlockSpec((1, 1, RWKV_SHIFT_W), lambda i, j: (i, 0, 0))
    else:
        shift_spec = pl.BlockSpec((1, ts, RWKV_SHIFT_W), lambda i, j: (i, j, 0))
    return pl.pallas_call(
        functools.partial(_rwkv_prep_kernel, ts=ts, seq=seq),
        out_shape=[out_sds] * 7,
        grid=(b, s // ts),
        in_specs=[
            pl.BlockSpec((1, ts, RW_BLOCK_W), lambda i, j: (i, j, 0)),
            pl.BlockSpec((1, SUBLANES, RW_BLOCK_W),
                         lambda i, j: (i, jnp.maximum(j * (ts // SUBLANES) - 1, 0), 0)),
            shift_spec,
            vec(RWKV_SHIFT_W), vec(MIX_W), vec(MIX_W), vec(MIX_W), vec(MIX_W), vec(MIX_W),
            pl.BlockSpec((LANES, MIX_W), row),
            pl.BlockSpec((LANES, MIX_W), row),
            pl.BlockSpec((MIX_W, MIX_W), row),
        ],
        out_specs=[out_spec] * 7,
        compiler_params=_cparams(("parallel", "parallel")),
        name="rwkv_prep",
    )(z3, z3, shift, p["mu"], p["w0"], p["a0"], p["k_k"], p["k_a"], p["r_k"],
      p["w_up"], p["a_up"], p["ones"])


def _rwkv_scan_kernel(r_ref, k_ref, v_ref, w_ref, kn_ref, b_ref, y_ref, sfin_ref, s_ref, *, tb):
    @pl.when(pl.program_id(0) == 0)
    def _():
        s_ref[...] = jnp.zeros_like(s_ref)

    rl = s_ref.shape[1]

    def row(ref, t, j):
        return jnp.broadcast_to(ref[t, j:j + 1, :], (rl, LANES))

    def token(t, carry):
        parts = [jnp.zeros((rl, LANES), F32) for _ in range(4)]
        for j in range(HEAD_DIM):
            parts[j % 4] = parts[j % 4] + s_ref[j] * row(kn_ref, t, j)
        sa = -((parts[0] + parts[1]) + (parts[2] + parts[3]))
        vt = v_ref[t]
        yparts = [jnp.zeros((rl, LANES), F32) for _ in range(4)]
        for j in range(HEAD_DIM):
            sj = s_ref[j] * row(w_ref, t, j) + sa * row(b_ref, t, j) + vt * row(k_ref, t, j)
            s_ref[j] = sj
            yparts[j % 4] = yparts[j % 4] + sj * row(r_ref, t, j)
        y_ref[t] = (yparts[0] + yparts[1]) + (yparts[2] + yparts[3])
        return carry

    lax.fori_loop(0, tb, token, 0)

    @pl.when(pl.program_id(0) == pl.num_programs(0) - 1)
    def _():
        sfin_ref[...] = s_ref[...]


def _rwkv_scan(rs, ks, vs, ws, kns, bs, tb):
    s = rs.shape[0]
    rl = vs.shape[1]
    jspec = pl.BlockSpec((tb, HEAD_DIM, LANES), lambda i: (i, 0, 0))
    ispec = pl.BlockSpec((tb, rl, LANES), lambda i: (i, 0, 0))
    return pl.pallas_call(
        functools.partial(_rwkv_scan_kernel, tb=tb),
        out_shape=[jax.ShapeDtypeStruct((s, rl, LANES), F32),
                   jax.ShapeDtypeStruct((HEAD_DIM, rl, LANES), F32)],
        grid=(s // tb,),
        in_specs=[jspec, jspec, ispec, jspec, jspec, jspec],
        out_specs=[ispec, pl.BlockSpec((HEAD_DIM, rl, LANES), lambda i: (0, 0, 0))],
        scratch_shapes=[pltpu.VMEM((HEAD_DIM, rl, LANES), F32)],
        compiler_params=_cparams(("arbitrary",)),
        name="rwkv_scan",
    )(rs, ks, vs, ws, kns, bs)


def _to_scan_j(x, ip):
    b, s, _ = x.shape
    t = x.reshape(b, s, N_HEADS, HEAD_DIM).transpose(1, 3, 0, 2).reshape(s, HEAD_DIM, b * N_HEADS)
    return jnp.concatenate([t] * ip, axis=-1)


def _to_scan_i(x, ip):
    b, s, _ = x.shape
    rl = HEAD_DIM // ip
    t = x.reshape(b, s, N_HEADS, ip, rl).transpose(1, 4, 3, 0, 2)
    return t.reshape(s, rl, ip * b * N_HEADS)


def _from_scan_i(y, b, ip):
    s, rl, _ = y.shape
    t = y.reshape(s, rl, ip, b, N_HEADS).transpose(3, 0, 4, 2, 1)
    return t.reshape(b, s, MIX_W)


def _state_from_scan(sf, b, ip):
    rl = sf.shape[1]
    t = sf.reshape(HEAD_DIM, rl, ip, b, N_HEADS).transpose(3, 4, 2, 1, 0)
    return t.reshape(b, N_HEADS, HEAD_DIM, HEAD_DIM)


def _rwkv_step_kernel(s_ref, r_ref, k_ref, v_ref, w_ref, kn_ref, b_ref, snew_ref, y_ref):
    for h in range(N_HEADS):
        s = s_ref[0, h]
        sa = -jnp.sum(s * kn_ref[0, h], axis=-1, keepdims=True)
        sn = s * w_ref[0, h] + sa * b_ref[0, h] + v_ref[0, h] * k_ref[0, h]
        snew_ref[0, h] = sn
        y_ref[0, h] = jnp.sum(sn * r_ref[0, h], axis=-1, keepdims=True)


def _rwkv_step(state, r, k, v, w, kn, bb):
    b = state.shape[0]
    jrow = lambda x: x.reshape(b, N_HEADS, 1, HEAD_DIM)
    jspec = pl.BlockSpec((1, N_HEADS, 1, HEAD_DIM), lambda i: (i, 0, 0, 0))
    cspec = pl.BlockSpec((1, N_HEADS, HEAD_DIM, 1), lambda i: (i, 0, 0, 0))
    sspec = pl.BlockSpec((1, N_HEADS, HEAD_DIM, HEAD_DIM), lambda i: (i, 0, 0, 0))
    snew, y = pl.pallas_call(
        _rwkv_step_kernel,
        out_shape=[jax.ShapeDtypeStruct(state.shape, F32),
                   jax.ShapeDtypeStruct((b, N_HEADS, HEAD_DIM, 1), F32)],
        grid=(b,),
        in_specs=[sspec, jspec, jspec, cspec, jspec, jspec, jspec],
        out_specs=[sspec, cspec],
        compiler_params=_cparams(("parallel",)),
        name="rwkv_step",
    )(state, jrow(r), jrow(k), v.reshape(b, N_HEADS, HEAD_DIM, 1), jrow(w), jrow(kn), jrow(bb))
    return snew, y.reshape(b, 1, MIX_W)


def _rwkv_post_kernel(y_ref, bonus_ref, g_ref, lnw_ref, lnb_ref, ones_ref, o_ref):
    y = y_ref[0]
    inv = 1.0 / HEAD_DIM
    mean = _dot3(y, ones_ref[...]) * inv
    yc = y - mean
    var = _dot3(yc * yc, ones_ref[...]) * inv
    yn = yc * lax.rsqrt(var + GN_EPS) * lnw_ref[...] + lnb_ref[...]
    o_ref[0] = (yn + bonus_ref[0]) * _silu(g_ref[0])


def _rwkv_post(y, bonus, z3, p, ts):
    b, s, _ = y.shape
    blk = pl.BlockSpec((1, ts, MIX_W), lambda i, j: (i, j, 0))
    row = lambda i, j: (0, 0)
    return pl.pallas_call(
        _rwkv_post_kernel,
        out_shape=jax.ShapeDtypeStruct((b, s, MIX_W), F32),
        grid=(b, s // ts),
        in_specs=[blk, blk,
                  pl.BlockSpec((1, ts, MIX_W), lambda i, j: (i, j, C_GRW // MIX_W)),
                  pl.BlockSpec((1, MIX_W), row), pl.BlockSpec((1, MIX_W), row),
                  pl.BlockSpec((MIX_W, MIX_W), row)],
        out_specs=blk,
        compiler_params=_cparams(("parallel", "parallel")),
        name="rwkv_post",
    )(y, bonus, z3, p["ln_w"], p["ln_b"], p["ones"])


def _logf(x, fb):
    u = x + fb
    return jnp.minimum(u, 0.0) - jnp.log(1.0 + jnp.exp(-jnp.abs(u)))


def _logf_kernel(f_ref, fb_ref, tri_ref, lf_ref, cum_ref, cumt_ref, carry_ref):
    @pl.when(pl.program_id(1) == 0)
    def _():
        carry_ref[...] = jnp.zeros_like(carry_ref)

    lf = _logf(f_ref[0], fb_ref[...])
    lf_ref[0] = lf
    hi, mid, lo = _split3(lf)
    tri = tri_ref[...]
    cum = _dot(tri, hi) + _dot(tri, mid) + _dot(tri, lo) + carry_ref[...]
    cum_ref[0] = cum
    cumt_ref[0] = cum.T
    carry_ref[...] = cum[cum.shape[0] - 1:, :]


def _fox_logf(z3, fbias_row, ts):
    b, s, _ = z3.shape
    r = lax.broadcasted_iota(jnp.int32, (ts, ts), 0)
    c = lax.broadcasted_iota(jnp.int32, (ts, ts), 1)
    tri = (c <= r).astype(BF16)
    blk = pl.BlockSpec((1, ts, LANES), lambda i, j: (i, j, 0))
    return pl.pallas_call(
        _logf_kernel,
        out_shape=[jax.ShapeDtypeStruct((b, s, LANES), F32),
                   jax.ShapeDtypeStruct((b, s, LANES), F32),
                   jax.ShapeDtypeStruct((b, LANES, s), F32)],
        grid=(b, s // ts),
        in_specs=[pl.BlockSpec((1, ts, LANES), lambda i, j: (i, j, C_FXF // LANES)),
                  pl.BlockSpec((1, LANES), lambda i, j: (0, 0)),
                  pl.BlockSpec((ts, ts), lambda i, j: (0, 0))],
        out_specs=[blk, blk, pl.BlockSpec((1, LANES, ts), lambda i, j: (i, 0, j))],
        scratch_shapes=[pltpu.VMEM((1, LANES), F32)],
        compiler_params=_cparams(("parallel", "arbitrary")),
        name="fox_logf",
    )(z3, fbias_row, tri)


def _head_masked_q(q_ref, h):
    g = h // 2
    q = q_ref[0, :, g * LANES:(g + 1) * LANES] * ATTN_SCALE
    if q.shape[0] < SUBLANES:
        q = jnp.broadcast_to(q[0:1], (SUBLANES, LANES))
    lane = lax.broadcasted_iota(jnp.int32, q.shape, 1)
    keep = (lane >= HEAD_DIM) if (h % 2) else (lane < HEAD_DIM)
    return jnp.where(keep, q, 0.0).astype(BF16)


def _pair_select(lo_half, hi_half):
    lane = lax.broadcasted_iota(jnp.int32, lo_half.shape, 1)
    return jnp.where(lane < HEAD_DIM, lo_half, hi_half)


def _pair_q(q_ref, g):
    q = q_ref[0, :, g * LANES:(g + 1) * LANES] * ATTN_SCALE
    lane = lax.broadcasted_iota(jnp.int32, q.shape, 1)
    return jnp.concatenate([jnp.where(lane < HEAD_DIM, q, 0.0),
                            jnp.where(lane >= HEAD_DIM, q, 0.0)], axis=0).astype(BF16)


def _lane_tile(x, width):
    return jnp.concatenate([x] * (width // LANES), axis=1)


def _tri_steps(nq, reverse):
    qs, ks = [], []
    for q in range(nq):
        for k in (range(q, -1, -1) if reverse else range(q + 1)):
            qs.append(q)
            ks.append(k)
    return jnp.asarray(np.array(qs, np.int32)), jnp.asarray(np.array(ks, np.int32))


def _sb_prompt_kernel(qi_ref, kv_ref, q_ref, k_ref, v_ref, g_ref, tri_ref, o_ref, acc_ref, carry_ref,
                      *, t):
    step = pl.program_id(1)
    qi = qi_ref[step]
    kv = kv_ref[step]

    def block(diagonal):
        kb = k_ref[0].astype(BF16)
        vb = v_ref[0].astype(BF16)
        if diagonal:
            mask = (lax.broadcasted_iota(jnp.int32, (t, t), 1)
                    < lax.broadcasted_iota(jnp.int32, (t, t), 0))
        zs = []
        for g in range(N_HEADS // 2):
            zz = _dot_nt(_pair_q(q_ref, g), kb[:, g * LANES:(g + 1) * LANES])
            zs += [zz[:t], zz[t:]]
        sps = [_softplus(z) for z in zs]
        if diagonal:
            sps = [jnp.where(mask, sp, 0.0) for sp in sps]
        sp_all = jnp.concatenate(sps, axis=0)
        incl = _dot(sp_all.astype(BF16), tri_ref[...])
        for g in range(N_HEADS // 2):
            aa = []
            for h in (2 * g, 2 * g + 1):
                inc_h = incl[h * t:(h + 1) * t]
                c = carry_ref[h]
                a = jnp.exp(zs[h] - inc_h - _lane_tile(c, t))
                if diagonal:
                    a = jnp.where(mask, a, 0.0)
                aa.append(a.astype(BF16))
                carry_ref[h] = c + jnp.broadcast_to(inc_h[:, 0:1], (t, LANES))
            oo = _dot(jnp.concatenate(aa, axis=0), vb[:, g * LANES:(g + 1) * LANES])
            acc_ref[:, g * LANES:(g + 1) * LANES] += _pair_select(oo[:t], oo[t:])

    @pl.when(kv == qi)
    def _():
        acc_ref[...] = jnp.zeros_like(acc_ref)
        carry_ref[...] = jnp.zeros_like(carry_ref)
        block(True)

    @pl.when(kv < qi)
    def _():
        block(False)

    @pl.when(kv == 0)
    def _():
        o_ref[0] = acc_ref[...] * _silu(g_ref[0])


def _sb_prompt(z3, t):
    b, s, _ = z3.shape
    r = lax.broadcasted_iota(jnp.int32, (t, t), 0)
    c = lax.broadcasted_iota(jnp.int32, (t, t), 1)
    tri = (r >= c).astype(BF16)
    qs, ks = _tri_steps(s // t, True)
    qblk = lambda col: pl.BlockSpec((1, t, MIX_W), lambda i, n, qa, ka: (i, qa[n], col // MIX_W))
    kblk = lambda col: pl.BlockSpec((1, t, MIX_W), lambda i, n, qa, ka: (i, ka[n], col // MIX_W))
    grid_spec = pltpu.PrefetchScalarGridSpec(
        num_scalar_prefetch=2,
        grid=(b, qs.shape[0]),
        in_specs=[qblk(C_SBQ), kblk(C_SBK), kblk(C_SBV), qblk(C_GSB),
                  pl.BlockSpec(tri.shape, lambda i, n, qa, ka: (0, 0))],
        out_specs=pl.BlockSpec((1, t, MIX_W), lambda i, n, qa, ka: (i, qa[n], 0)),
        scratch_shapes=[pltpu.VMEM((t, MIX_W), F32), pltpu.VMEM((N_HEADS, t, LANES), F32)],
    )
    return pl.pallas_call(
        functools.partial(_sb_prompt_kernel, t=t),
        out_shape=jax.ShapeDtypeStruct((b, s, MIX_W), F32),
        grid_spec=grid_spec,
        compiler_params=_cparams(("parallel", "arbitrary")),
        name="sb_prompt",
    )(qs, ks, z3, z3, z3, z3, tri)


def _fox_prompt_kernel(qi_ref, kv_ref, q_ref, k_ref, v_ref, g_ref, fq_ref, fk_ref, o_ref,
                       acc_ref, m_ref, l_ref, fqb_ref, *, t):
    step = pl.program_id(1)
    qi = qi_ref[step]
    kv = kv_ref[step]

    @pl.when(kv == 0)
    def _():
        acc_ref[...] = jnp.zeros_like(acc_ref)
        m_ref[...] = jnp.full_like(m_ref, -jnp.inf)
        l_ref[...] = jnp.zeros_like(l_ref)
        fq = fq_ref[0]
        for h in range(N_HEADS):
            fqb_ref[h] = jnp.broadcast_to(fq[:, h:h + 1], (t, LANES))

    def block(diagonal):
        kb = k_ref[0].astype(BF16)
        vb = v_ref[0].astype(BF16)
        fk = fk_ref[0]
        if diagonal:
            mask = (lax.broadcasted_iota(jnp.int32, (t, t), 1)
                    <= lax.broadcasted_iota(jnp.int32, (t, t), 0))
        for g in range(N_HEADS // 2):
            zz = _dot_nt(_pair_q(q_ref, g), kb[:, g * LANES:(g + 1) * LANES])
            ps, alphas = [], []
            for idx, h in enumerate((2 * g, 2 * g + 1)):
                z = zz[idx * t:(idx + 1) * t] + _lane_tile(fqb_ref[h], t) - fk[h:h + 1, :]
                if diagonal:
                    z = jnp.where(mask, z, -jnp.inf)
                m_old = m_ref[h]
                m_new = jnp.maximum(
                    m_old, jnp.broadcast_to(jnp.max(z, axis=-1, keepdims=True), (t, LANES)))
                alpha = jnp.exp(m_old - m_new)
                p = jnp.exp(z - _lane_tile(m_new, t))
                l_ref[h] = l_ref[h] * alpha + jnp.broadcast_to(
                    jnp.sum(p, axis=-1, keepdims=True), (t, LANES))
                m_ref[h] = m_new
                ps.append(p.astype(BF16))
                alphas.append(alpha)
            oo = _dot(jnp.concatenate(ps, axis=0), vb[:, g * LANES:(g + 1) * LANES])
            sl = slice(g * LANES, (g + 1) * LANES)
            acc_ref[:, sl] = (acc_ref[:, sl] * _pair_select(alphas[0], alphas[1])
                              + _pair_select(oo[:t], oo[t:]))

    @pl.when(kv < qi)
    def _():
        block(False)

    @pl.when(kv == qi)
    def _():
        block(True)
        for g in range(N_HEADS // 2):
            sl = slice(g * LANES, (g + 1) * LANES)
            inv = _pair_select(1.0 / l_ref[2 * g], 1.0 / l_ref[2 * g + 1])
            o_ref[0, :, sl] = acc_ref[:, sl] * inv * _silu(g_ref[0, :, sl])


def _fox_prompt(z3, cum, cumt, t):
    b, s, _ = z3.shape
    qs, ks = _tri_steps(s // t, False)
    qblk = lambda col: pl.BlockSpec((1, t, MIX_W), lambda i, n, qa, ka: (i, qa[n], col // MIX_W))
    kblk = lambda col: pl.BlockSpec((1, t, MIX_W), lambda i, n, qa, ka: (i, ka[n], col // MIX_W))
    stat = pltpu.VMEM((N_HEADS, t, LANES), F32)
    grid_spec = pltpu.PrefetchScalarGridSpec(
        num_scalar_prefetch=2,
        grid=(b, qs.shape[0]),
        in_specs=[qblk(C_FXQ), kblk(C_FXK), kblk(C_FXV), qblk(C_GFX),
                  pl.BlockSpec((1, t, LANES), lambda i, n, qa, ka: (i, qa[n], 0)),
                  pl.BlockSpec((1, SUBLANES, t), lambda i, n, qa, ka: (i, 0, ka[n]))],
        out_specs=pl.BlockSpec((1, t, MIX_W), lambda i, n, qa, ka: (i, qa[n], 0)),
        scratch_shapes=[pltpu.VMEM((t, MIX_W), F32), stat, stat, stat],
    )
    return pl.pallas_call(
        functools.partial(_fox_prompt_kernel, t=t),
        out_shape=jax.ShapeDtypeStruct((b, s, MIX_W), F32),
        grid_spec=grid_spec,
        compiler_params=_cparams(("parallel", "arbitrary")),
        name="fox_prompt",
    )(qs, ks, z3, z3, z3, z3, cum, cumt)


def _mem_kernel(q_ref, g_ref, mk_ref, mv_ref, o_ref):
    kb = mk_ref[0].astype(BF16)
    vb = mv_ref[0].astype(BF16)
    for g in range(MEM_HEADS // 2):
        outs = []
        for h in (2 * g, 2 * g + 1):
            z = _dot_nt(_head_masked_q(q_ref, h), kb[:, g * LANES:(g + 1) * LANES])
            m = jnp.max(z, axis=-1, keepdims=True)
            e = jnp.exp(z - m)
            p = e / jnp.sum(e, axis=-1, keepdims=True)
            outs.append(_dot(p.astype(BF16), vb[:, g * LANES:(g + 1) * LANES]))
        sl = slice(g * LANES, (g + 1) * LANES)
        o = _pair_select(outs[0], outs[1])
        o_ref[0, :, sl] = o[:o_ref.shape[1]] * _silu(g_ref[0, :, sl])


def _mem_attend(z3, mk, mv, tq):
    b, s, _ = z3.shape
    n_mem = mk.shape[1]
    kv = pl.BlockSpec((1, n_mem, MEM_W), lambda i, j: (i, 0, 0))
    return pl.pallas_call(
        _mem_kernel,
        out_shape=jax.ShapeDtypeStruct((b, s, MEM_W), F32),
        grid=(b, s // tq),
        in_specs=[pl.BlockSpec((1, tq, MEM_W), lambda i, j: (i, j, C_MQ // MEM_W)),
                  pl.BlockSpec((1, tq, MEM_W), lambda i, j: (i, j, C_GM // MEM_W)),
                  kv, kv],
        out_specs=pl.BlockSpec((1, tq, MEM_W), lambda i, j: (i, j, 0)),
        compiler_params=_cparams(("parallel", "parallel")),
        name="mem_attend",
    )(z3, z3, mk, mv)


def _merge_kernel(x_ref, yrw_ref, ysb_ref, yfx_ref, ym_ref, g0_ref, g1_ref, g2_ref, g3_ref,
                  wrw_ref, wsb_ref, wfx_ref, wm_ref, wout_ref, fg_ref, xo_ref, yo_ref):
    def branch(y_ref, w_ref, gate_ref):
        return _sigmoid(gate_ref[0]) * _dot(y_ref[0].astype(BF16), w_ref[...])

    merged = (branch(yrw_ref, wrw_ref, g0_ref) + branch(ysb_ref, wsb_ref, g1_ref)
              + branch(yfx_ref, wfx_ref, g2_ref) + branch(ym_ref, wm_ref, g3_ref))
    xn = x_ref[0] + _dot(merged.astype(BF16), wout_ref[...])
    xo_ref[0] = xn
    ms = jnp.mean(xn * xn, axis=-1, keepdims=True)
    yo_ref[0] = xn * lax.rsqrt(ms + NORM_EPS) * fg_ref[...]


def _merge(x3, yrw, ysb, yfx, ym, z3, wts, final_g, tm):
    b, s, d = x3.shape
    row = lambda i, j: (0, 0)
    blk = lambda w: pl.BlockSpec((1, tm, w), lambda i, j: (i, j, 0))
    gate = lambda n: pl.BlockSpec((1, tm, d), lambda i, j: (i, j, C_GATE // d + n))
    full = lambda a: pl.BlockSpec(a.shape, row)
    return pl.pallas_call(
        _merge_kernel,
        out_shape=[jax.ShapeDtypeStruct(x3.shape, F32)] * 2,
        grid=(b, s // tm),
        in_specs=[blk(d), blk(MIX_W), blk(MIX_W), blk(MIX_W), blk(MEM_W),
                  gate(0), gate(1), gate(2), gate(3),
                  full(wts["rw"]), full(wts["sb"]), full(wts["fx"]), full(wts["m"]),
                  full(wts["out"]), pl.BlockSpec((1, d), row)],
        out_specs=[blk(d), blk(d)],
        compiler_params=_cparams(("parallel", "parallel")),
        name="merge",
    )(x3, yrw, ysb, yfx, ym, z3, z3, z3, z3,
      wts["rw"], wts["sb"], wts["fx"], wts["m"], wts["out"], final_g.reshape(1, d))


def _block_diag_q(q_row):
    q = jnp.broadcast_to(q_row * ATTN_SCALE, (N_HEADS, MIX_W))
    return jnp.where(_head_rows(), q, 0.0).astype(BF16)


def _head_rows():
    r = lax.broadcasted_iota(jnp.int32, (N_HEADS, MIX_W), 0)
    c = lax.broadcasted_iota(jnp.int32, (N_HEADS, MIX_W), 1) // HEAD_DIM
    return r == c


def _collapse_heads(acc):
    return jnp.sum(jnp.where(_head_rows(), acc, 0.0), axis=0, keepdims=True)


def _page_group(n_pages):
    return next(g for g in (8, 4, 2, 1) if n_pages % g == 0)


def _pages_keys_minor(cache):
    l, n, page = cache.shape[:3]
    return jnp.transpose(cache, (0, 1, 3, 4, 2)).reshape(l, n, -1, page)


def _pool_specs(grp, n_pages, layer, rows, page):
    def spec(g):
        return pl.BlockSpec((None, None, rows, page),
                            lambda i, p, pt: (layer, pt[i, n_pages - 1 - (p * grp + g)], 0, 0))
    return [spec(g) for g in range(grp)]


def _suffix_total_matrix(pieces, page, inclusive):
    r = lax.broadcasted_iota(jnp.int32, (pieces * page, 2 * page), 0) % page
    c = lax.broadcasted_iota(jnp.int32, (pieces * page, 2 * page), 1)
    return jnp.where(c < page, (r >= c) if inclusive else (r > c), True).astype(BF16)


def _sb_decode_kernel(pt_ref, q_ref, kn_ref, vn_ref, g_ref, *rest, page, grp):
    k_refs, v_refs = rest[:grp], rest[grp:2 * grp]
    tri_ref, o_ref, qbd_ref, acc_ref, carry_ref = rest[2 * grp:]
    p = pl.program_id(1)
    n_steps = pl.num_programs(1)

    @pl.when(p == 0)
    def _():
        qbd = _block_diag_q(q_ref[0])
        qbd_ref[...] = qbd
        new_pos = n_steps * (grp * page)
        new_mask = new_pos < new_pos
        z_new = jnp.sum(qbd.astype(F32) * kn_ref[0].astype(BF16).astype(F32), axis=-1, keepdims=True)
        sp_new = jnp.where(new_mask, _softplus(z_new), 0.0)
        a_new = jnp.where(new_mask, jnp.exp(z_new - sp_new), 0.0)
        acc_ref[...] = a_new.astype(BF16).astype(F32) * vn_ref[0].astype(BF16).astype(F32)
        carry_ref[...] = jnp.broadcast_to(sp_new, carry_ref.shape)

    qbd = qbd_ref[...]
    zs = [_dot(qbd, k_refs[g][...].astype(BF16)) for g in range(grp)]
    sp_all = jnp.concatenate([_softplus(z) for z in zs], axis=0)
    hi = sp_all.astype(BF16)
    lo = (sp_all - hi.astype(F32)).astype(BF16)
    both = _dot(jnp.concatenate([hi, lo], axis=1), tri_ref[...])
    carry = carry_ref[...]
    acc = acc_ref[...]
    for g in range(grp):
        rows = slice(g * N_HEADS, (g + 1) * N_HEADS)
        a = jnp.exp(zs[g] - both[rows, :page] - carry)
        acc = acc + _dot_nt(a.astype(BF16), v_refs[g][...].astype(BF16))
        carry = carry + both[rows, page:]
    acc_ref[...] = acc
    carry_ref[...] = carry

    @pl.when(p == n_steps - 1)
    def _():
        o_ref[0] = _collapse_heads(acc_ref[...]) * _silu(g_ref[0])


def _sb_decode(zs3, cache_k, cache_v, page_table, layer):
    b = zs3.shape[0]
    n_pages = page_table.shape[1]
    page = cache_k.shape[2]
    grp = _page_group(n_pages)
    row = lambda col: pl.BlockSpec((1, 1, MIX_W), lambda i, p, pt: (i, 0, col // MIX_W))
    pools = _pool_specs(grp, n_pages, layer, MIX_W, page)
    tri = _suffix_total_matrix(2, page, True)
    grid_spec = pltpu.PrefetchScalarGridSpec(
        num_scalar_prefetch=1,
        grid=(b, n_pages // grp),
        in_specs=[row(C_SBQ), row(C_SBK), row(C_SBV), row(C_GSB)] + pools + pools
                 + [pl.BlockSpec(tri.shape, lambda i, p, pt: (0, 0))],
        out_specs=pl.BlockSpec((1, 1, MIX_W), lambda i, p, pt: (i, 0, 0)),
        scratch_shapes=[pltpu.VMEM((N_HEADS, MIX_W), BF16), pltpu.VMEM((N_HEADS, MIX_W), F32),
                        pltpu.VMEM((N_HEADS, page), F32)],
    )
    kt = _pages_keys_minor(cache_k)
    vt = _pages_keys_minor(cache_v)
    return pl.pallas_call(
        functools.partial(_sb_decode_kernel, page=page, grp=grp),
        out_shape=jax.ShapeDtypeStruct((b, 1, MIX_W), F32),
        grid_spec=grid_spec,
        compiler_params=_cparams(("parallel", "arbitrary")),
        name="sb_decode",
    )(page_table, zs3, zs3, zs3, zs3, *([kt] * grp), *([vt] * grp), tri)


def _fox_decode_kernel(pt_ref, q_ref, kn_ref, vn_ref, g_ref, f_ref, fb_ref, *rest, page, grp):
    k_refs, v_refs, lf_refs = rest[:grp], rest[grp:2 * grp], rest[2 * grp:3 * grp]
    tri_ref, o_ref, lf_ref, qbd_ref, acc_ref, carry_ref, m_ref, l_ref = rest[3 * grp:]
    p = pl.program_id(1)
    n_steps = pl.num_programs(1)

    @pl.when(p == 0)
    def _():
        qbd = _block_diag_q(q_ref[0])
        qbd_ref[...] = qbd
        lf_new = _logf(f_ref[0], fb_ref[...])
        lf_ref[0] = lf_new
        r = lax.broadcasted_iota(jnp.int32, (N_HEADS, LANES), 0)
        c = lax.broadcasted_iota(jnp.int32, (N_HEADS, LANES), 1)
        lf_col = jnp.sum(jnp.where(r == c, jnp.broadcast_to(lf_new, (N_HEADS, LANES)), 0.0),
                         axis=-1, keepdims=True)
        carry_ref[...] = jnp.broadcast_to(lf_col, carry_ref.shape)
        z_new = jnp.sum(qbd.astype(F32) * kn_ref[0].astype(BF16).astype(F32), axis=-1, keepdims=True)
        m_ref[...] = z_new
        l_ref[...] = jnp.ones_like(l_ref)
        acc_ref[...] = jnp.broadcast_to(vn_ref[0].astype(BF16).astype(F32), acc_ref.shape)

    lf_all = jnp.concatenate([lf_refs[g][...] for g in range(grp)], axis=0)
    both = _dot(jnp.concatenate(_split3(lf_all), axis=1), tri_ref[...])
    qbd = qbd_ref[...]
    carry = carry_ref[...]
    zs = []
    for g in range(grp):
        rows = slice(g * N_HEADS, (g + 1) * N_HEADS)
        zs.append(_dot(qbd, k_refs[g][...].astype(BF16)) + both[rows, :page] + carry)
        carry = carry + both[rows, page:]
    carry_ref[...] = carry
    z = jnp.concatenate(zs, axis=1)
    m_old = m_ref[...]
    m_new = jnp.maximum(m_old, jnp.max(z, axis=-1, keepdims=True))
    alpha = jnp.exp(m_old - m_new)
    pr = jnp.exp(z - m_new)
    l_ref[...] = l_ref[...] * alpha + jnp.sum(pr, axis=-1, keepdims=True)
    m_ref[...] = m_new
    acc = acc_ref[...] * alpha
    for g in range(grp):
        acc = acc + _dot_nt(pr[:, g * page:(g + 1) * page].astype(BF16),
                            v_refs[g][...].astype(BF16))
    acc_ref[...] = acc

    @pl.when(p == n_steps - 1)
    def _():
        o_ref[0] = _collapse_heads(acc_ref[...] / l_ref[...]) * _silu(g_ref[0])


def _fox_decode(zs3, cache_k, cache_v, cache_logf, fbias_row, page_table, layer):
    b = zs3.shape[0]
    n_pages = page_table.shape[1]
    page = cache_k.shape[2]
    grp = _page_group(n_pages)
    row = lambda col: pl.BlockSpec((1, 1, MIX_W), lambda i, p, pt: (i, 0, col // MIX_W))
    pools = _pool_specs(grp, n_pages, layer, MIX_W, page)
    tri = _suffix_total_matrix(3, page, False)
    grid_spec = pltpu.PrefetchScalarGridSpec(
        num_scalar_prefetch=1,
        grid=(b, n_pages // grp),
        in_specs=[row(C_FXQ), row(C_FXK), row(C_FXV), row(C_GFX),
                  pl.BlockSpec((1, 1, LANES), lambda i, p, pt: (i, 0, C_FXF // LANES)),
                  pl.BlockSpec((1, LANES), lambda i, p, pt: (0, 0))]
                 + pools + pools + _pool_specs(grp, n_pages, layer, N_HEADS, page)
                 + [pl.BlockSpec(tri.shape, lambda i, p, pt: (0, 0))],
        out_specs=[pl.BlockSpec((1, 1, MIX_W), lambda i, p, pt: (i, 0, 0)),
                   pl.BlockSpec((1, 1, LANES), lambda i, p, pt: (i, 0, 0))],
        scratch_shapes=[pltpu.VMEM((N_HEADS, MIX_W), BF16), pltpu.VMEM((N_HEADS, MIX_W), F32),
                        pltpu.VMEM((N_HEADS, page), F32),
                        pltpu.VMEM((N_HEADS, 1), F32), pltpu.VMEM((N_HEADS, 1), F32)],
    )
    kt = _pages_keys_minor(cache_k)
    vt = _pages_keys_minor(cache_v)
    lft = jnp.transpose(cache_logf, (0, 1, 3, 2))
    return pl.pallas_call(
        functools.partial(_fox_decode_kernel, page=page, grp=grp),
        out_shape=[jax.ShapeDtypeStruct((b, 1, MIX_W), F32),
                   jax.ShapeDtypeStruct((b, 1, LANES), F32)],
        grid_spec=grid_spec,
        compiler_params=_cparams(("parallel", "arbitrary")),
        name="fox_decode",
    )(page_table, zs3, zs3, zs3, zs3, zs3, fbias_row,
      *([kt] * grp), *([vt] * grp), *([lft] * grp), tri)


def _reorder_w_in(w):
    c_grw = RWKV_SHIFT_W
    c_fxf = c_grw + 9 * MIX_W
    c_mq = c_fxf + N_HEADS
    pad = jnp.zeros((w.shape[0], RW_BLOCK_W - RWKV_SHIFT_W - N_HEADS), w.dtype)
    return jnp.concatenate(
        [w[:, :RWKV_SHIFT_W], w[:, c_fxf:c_mq], pad, w[:, c_grw:c_fxf], w[:, c_mq:]],
        axis=1).astype(BF16)


def _pick(n, pref):
    return pref if n % pref == 0 else n


def _heads(x, b, s):
    return x.reshape(b, s, -1, HEAD_DIM)


def kernel(x_prompt, x_sample, cache_sb_k, cache_sb_v, cache_fox_k, cache_fox_v, cache_fox_logf, cache_mem_k, cache_mem_v, state_rwkv_wkv, state_rwkv_shift, page_table, mem_prompt, norm_g, w_in, rwkv_mu, rwkv_w0, rwkv_w_up, rwkv_a0, rwkv_a_up, rwkv_k_k, rwkv_k_a, rwkv_r_k, rwkv_ln_w, rwkv_ln_b, fox_fbias, mem_norm_g, w_mem_kv, w_rwkv_o, w_sb_o, w_fox_o, w_mem_o, w_out, final_norm_g):
    depth = w_in.shape[0]
    bp, sp, d = x_prompt.shape
    bs, ss, _ = x_sample.shape
    assert ss == 1, "the sample group carries one new token per sequence"
    n_mem = mem_prompt.shape[1]
    bh = bp * N_HEADS
    ip = LANES // bh
    ones = _head_ones(MIX_W)
    zero_shift = jnp.zeros((bp, 1, RWKV_SHIFT_W), F32)

    hp, hs = x_prompt, x_sample
    yp = ys = None
    p_new = [[] for _ in range(9)]
    s_new = [[] for _ in range(7)]
    for l in range(depth):
        w_l = _reorder_w_in(w_in[l])
        rp = {
            "mu": rwkv_mu[l].reshape(1, -1), "w0": rwkv_w0[l].reshape(1, -1),
            "a0": rwkv_a0[l].reshape(1, -1), "k_k": rwkv_k_k[l].reshape(1, -1),
            "k_a": rwkv_k_a[l].reshape(1, -1), "r_k": rwkv_r_k[l].reshape(1, -1),
            "ln_w": rwkv_ln_w[l].reshape(1, -1), "ln_b": rwkv_ln_b[l].reshape(1, -1),
            "w_up": jnp.concatenate([rwkv_w_up[l], jnp.zeros_like(rwkv_a_up[l])], 0).astype(BF16),
            "a_up": jnp.concatenate([jnp.zeros_like(rwkv_w_up[l]), rwkv_a_up[l]], 0).astype(BF16),
            "ones": ones,
        }
        wts = {"rw": w_rwkv_o[l].astype(BF16), "sb": w_sb_o[l].astype(BF16),
               "fx": w_fox_o[l].astype(BF16), "m": w_mem_o[l].astype(BF16),
               "out": w_out[l].astype(BF16)}
        fbias_row = jnp.zeros((1, LANES), F32).at[0, :N_HEADS].set(fox_fbias[l])

        kvm = _norm_proj(mem_prompt.reshape(bp * n_mem, d), mem_norm_g[l],
                         w_mem_kv[l].astype(BF16), 2 * MEM_W).reshape(bp, n_mem, 2 * MEM_W)
        mk, mv = kvm[..., :MEM_W], kvm[..., MEM_W:]
        z = _norm_proj(hp.reshape(bp * sp, d), norm_g[l], w_l, 1024).reshape(bp, sp, N_PROJ)
        ts = _pick(sp, 256)
        r_, k_, v_, w_, kn_, b_, bonus = _rwkv_prep(z, zero_shift, rp, ts, True)
        y_s, s_fin = _rwkv_scan(_to_scan_j(r_, ip), _to_scan_j(k_, ip), _to_scan_i(v_, ip),
                                _to_scan_j(w_, ip), _to_scan_j(kn_, ip), _to_scan_j(b_, ip),
                                _pick(sp, 32))
        y_rw = _rwkv_post(_from_scan_i(y_s, bp, ip), bonus, z, rp, ts)
        y_sb = _sb_prompt(z, ts)
        logf, cum, cumt = _fox_logf(z, fbias_row, ts)
        y_fx = _fox_prompt(z, cum, cumt, _pick(sp, 512))
        y_m = _mem_attend(z, mk, mv, ts)
        hp, yp = _merge(hp, y_rw, y_sb, y_fx, y_m, z, wts, final_norm_g, ts)
        st = (_heads(z[..., C_SBK:C_SBK + MIX_W], bp, sp), _heads(z[..., C_SBV:C_SBV + MIX_W], bp, sp),
              _heads(z[..., C_FXK:C_FXK + MIX_W], bp, sp), _heads(z[..., C_FXV:C_FXV + MIX_W], bp, sp),
              logf[..., :N_HEADS], _state_from_scan(s_fin, bp, ip), z[:, -1, :RWKV_SHIFT_W],
              _heads(mk, bp, n_mem), _heads(mv, bp, n_mem))
        for lst, arr in zip(p_new, st):
            lst.append(arr)

        zs = _norm_proj(hs.reshape(bs, d), norm_g[l], w_l, 1024)
        zs_rows = zs.reshape(1, bs, N_PROJ)
        zs_seq = zs.reshape(bs, 1, N_PROJ)
        r_, k_, v_, w_, kn_, b_, bonus = _rwkv_prep(
            zs_rows, state_rwkv_shift[l].reshape(1, bs, RWKV_SHIFT_W), rp, bs, False)
        s_next, y_s = _rwkv_step(state_rwkv_wkv[l], r_, k_, v_, w_, kn_, b_)
        y_rw = _rwkv_post(y_s.reshape(1, bs, MIX_W), bonus, zs_rows, rp, bs)
        y_sb = _sb_decode(zs_seq, cache_sb_k, cache_sb_v, page_table, l)
        y_fx, logf_s = _fox_decode(zs_seq, cache_fox_k, cache_fox_v, cache_fox_logf, fbias_row,
                                   page_table, l)
        y_m = _mem_attend(zs_seq, cache_mem_k[l].reshape(bs, n_mem, MEM_W),
                          cache_mem_v[l].reshape(bs, n_mem, MEM_W), 1)
        hs_rows, ys_rows = _merge(hs.reshape(1, bs, d), y_rw, y_sb.reshape(1, bs, MIX_W),
                                  y_fx.reshape(1, bs, MIX_W), y_m.reshape(1, bs, MEM_W),
                                  zs_rows, wts, final_norm_g, bs)
        hs, ys = hs_rows.reshape(bs, 1, d), ys_rows.reshape(bs, 1, d)
        st = (_heads(zs_seq[..., C_SBK:C_SBK + MIX_W], bs, 1), _heads(zs_seq[..., C_SBV:C_SBV + MIX_W], bs, 1),
              _heads(zs_seq[..., C_FXK:C_FXK + MIX_W], bs, 1), _heads(zs_seq[..., C_FXV:C_FXV + MIX_W], bs, 1),
              logf_s[..., :N_HEADS], s_next, zs[:, :RWKV_SHIFT_W])
        for lst, arr in zip(s_new, st):
            lst.append(arr)

    return (yp, ys) + tuple(jnp.stack(t) for t in p_new) + tuple(jnp.stack(t) for t in s_new)
```
